```python
import math
import jax, jax.numpy as jnp
from jax import lax
import numpy as np

D_MODEL = 1024
BATCH = 8
SEQ = 2048
DEPTH = 1
DEC_BATCH = 128
DEC_SEQ = 8
PAST_LEN = 16384
PAGE_SIZE = 128

MIX_W = D_MODEL
LRU_W = MIX_W // 2
LRU_HEADS = 8
LRU_HD = LRU_W // LRU_HEADS
LRU_C = 8.0
CONV_W = 4
POOL_W = MIX_W - LRU_W
POOL_WINDOWS = (2, 4, 8, 16)
POOL_GROUPS = len(POOL_WINDOWS)
POOL_GW = POOL_W // POOL_GROUPS
POOL_BUF = max(POOL_WINDOWS) - 1
D_FF = ((8 * D_MODEL // 3 + 127) // 128) * 128
FFN_CONV_W = 3
EPS = 1e-6

kernel_name = "hybrid_rglru_pool_convffn_step"


def _rms(x, g):
    xf = x.astype(jnp.float32)
    y = xf * lax.rsqrt(jnp.mean(xf * xf, axis=-1, keepdims=True) + EPS)
    return (y * g.astype(jnp.float32)).astype(x.dtype)


def _causal_dwconv(u, buf, w, b):
    K = w.shape[0]
    T = u.shape[1]
    ext = jnp.concatenate([buf.astype(u.dtype), u], axis=1)
    out = b + sum(ext[:, k:k + T] * w[k] for k in range(K))
    return out.astype(u.dtype), ext[:, T:]


def _multiscale_pool(u, buf, pos0):
    T = u.shape[1]
    ext_raw = jnp.concatenate([buf.astype(u.dtype), u], axis=1)
    ext = ext_raw.astype(jnp.float32)
    cs = jnp.cumsum(ext, axis=1)
    cs = jnp.concatenate([jnp.zeros_like(cs[:, :1]), cs], axis=1)
    end = cs[:, POOL_BUF + 1:]
    pos = pos0 + jnp.arange(T)
    outs = []
    for g, w in enumerate(POOL_WINDOWS):
        sl = slice(g * POOL_GW, (g + 1) * POOL_GW)
        start = cs[:, POOL_BUF + 1 - w:POOL_BUF + 1 - w + T, sl]
        cnt = jnp.minimum(pos + 1, w).astype(jnp.float32)[None, :, None]
        outs.append((end[..., sl] - start) / cnt)
    mean = jnp.concatenate(outs, axis=-1)
    return (mean - u.astype(jnp.float32)).astype(u.dtype), ext_raw[:, T:]


def _rglru(xb, h0, w_a, b_a, w_i, b_i, lam):
    B, T, W = xb.shape
    xf = xb.astype(jnp.float32)
    xh = xf.reshape(B, T, LRU_HEADS, LRU_HD)
    r = jax.nn.sigmoid(jnp.einsum('bthi,hij->bthj', xh, w_a.astype(jnp.float32)).reshape(B, T, W) + b_a)
    i = jax.nn.sigmoid(jnp.einsum('bthi,hij->bthj', xh, w_i.astype(jnp.float32)).reshape(B, T, W) + b_i)
    log_a = -LRU_C * r * jax.nn.softplus(-lam.astype(jnp.float32))
    a = jnp.exp(log_a)
    mult = jnp.sqrt(-jnp.expm1(2.0 * log_a))
    bt = mult * (i * xf)
    bt = bt.at[:, 0].add(a[:, 0] * h0.astype(jnp.float32))

    def comb(left, right):
        a1, b1 = left
        a2, b2 = right
        return a1 * a2, a2 * b1 + b2

    _, h = lax.associative_scan(comb, (a, bt), axis=1)
    return h.astype(xb.dtype), h[:, -1].astype(h0.dtype)


def _layer(x, c, conv_buf, h0, pool_buf, ffn_buf, pos0,
           w_ada, b_ada, g_pre1, w_in, w_conv, b_conv, w_a, b_a, w_i, b_i, lam,
           w_pool, pool_scale, w_out, g_post1, g_pre2, w_up, w_fconv, b_fconv, w_down, g_post2):
    B, T, _ = x.shape
    mod = jnp.einsum('bd,de->be', jax.nn.silu(c), w_ada) + b_ada
    sh1, sc1, ga1, sh2, sc2, ga2 = jnp.split(mod[:, None, :], 6, axis=-1)
    h = _rms(x, g_pre1) * (1 + sc1) + sh1
    u = jnp.einsum('btd,de->bte', h, w_in)
    xb = u[..., :LRU_W]
    gb = u[..., LRU_W:2 * LRU_W]
    pb = u[..., 2 * LRU_W:]
    xc, new_conv = _causal_dwconv(xb, conv_buf, w_conv, b_conv)
    hr, h_last = _rglru(xc, h0, w_a, b_a, w_i, b_i, lam)
    lru_out = hr * jax.nn.gelu(gb)
    pz, new_pool = _multiscale_pool(pb, pool_buf, pos0)
    pz = jnp.einsum('btgi,gij->btgj', pz.reshape(B, T, POOL_GROUPS, POOL_GW), w_pool)
    pz = pz.reshape(B, T, POOL_W) * pool_scale
    m = jnp.einsum('bte,ed->btd', jnp.concatenate([lru_out, pz], axis=-1), w_out)
    x = x + ga1 * _rms(m, g_post1)
    h2 = _rms(x, g_pre2) * (1 + sc2) + sh2
    up = jnp.einsum('btd,df->btf', h2, w_up)
    upc, new_ffn = _causal_dwconv(up, ffn_buf, w_fconv, b_fconv)
    f = jax.nn.gelu(upc[..., :D_FF]) * upc[..., D_FF:]
    f = jnp.einsum('btf,fd->btd', f, w_down)
    x = x + ga2 * _rms(f, g_post2)
    return x, new_conv, h_last, new_pool, new_ffn


def setup_inputs(seed: int = 0) -> dict:
    key = jax.random.key(seed)
    ks = iter(jax.random.split(key, 40))
    f32 = jnp.float32

    def nrm(shape, s):
        return jax.random.normal(next(ks), shape, f32) * s

    L = DEPTH
    a0 = jax.random.uniform(next(ks), (L, LRU_W), f32, 0.9, 0.999)
    s = a0 ** (1.0 / LRU_C)
    lam = jnp.log(s) - jnp.log1p(-s)
    return {
        "x_prompt": nrm((BATCH, SEQ, D_MODEL), 1.0),
        "x_sample": nrm((DEC_BATCH, DEC_SEQ, D_MODEL), 1.0),
        "c_prompt": nrm((BATCH, D_MODEL), 1.0),
        "c_sample": nrm((DEC_BATCH, D_MODEL), 1.0),
        "state_conv": nrm((L, DEC_BATCH, CONV_W - 1, LRU_W), 1.0),
        "state_lru": nrm((L, DEC_BATCH, LRU_W), 0.5),
        "state_pool": nrm((L, DEC_BATCH, POOL_BUF, POOL_W), 1.0),
        "state_ffn_conv": nrm((L, DEC_BATCH, FFN_CONV_W - 1, 2 * D_FF), 1.0),
        "w_ada": nrm((L, D_MODEL, 6 * D_MODEL), D_MODEL ** -0.5),
        "b_ada": nrm((L, 6 * D_MODEL), 0.01),
        "g_pre1": 1.0 + nrm((L, D_MODEL), 0.1),
        "w_in": nrm((L, D_MODEL, 2 * LRU_W + POOL_W), D_MODEL ** -0.5),
        "w_conv": nrm((L, CONV_W, LRU_W), CONV_W ** -0.5),
        "b_conv": nrm((L, LRU_W), 0.01),
        "w_a": nrm((L, LRU_HEADS, LRU_HD, LRU_HD), LRU_HD ** -0.5),
        "b_a": nrm((L, LRU_W), 0.01),
        "w_i": nrm((L, LRU_HEADS, LRU_HD, LRU_HD), LRU_HD ** -0.5),
        "b_i": nrm((L, LRU_W), 0.01),
        "lam": lam,
        "w_pool": nrm((L, POOL_GROUPS, POOL_GW, POOL_GW), POOL_GW ** -0.5),
        "pool_scale": 1.0 + nrm((L, POOL_W), 0.1),
        "w_out": nrm((L, LRU_W + POOL_W, D_MODEL), (LRU_W + POOL_W) ** -0.5),
        "g_post1": 1.0 + nrm((L, D_MODEL), 0.1),
        "g_pre2": 1.0 + nrm((L, D_MODEL), 0.1),
        "w_up": nrm((L, D_MODEL, 2 * D_FF), D_MODEL ** -0.5),
        "w_fconv": nrm((L, FFN_CONV_W, 2 * D_FF), FFN_CONV_W ** -0.5),
        "b_fconv": nrm((L, 2 * D_FF), 0.01),
        "w_down": nrm((L, D_FF, D_MODEL), D_FF ** -0.5),
        "g_post2": 1.0 + nrm((L, D_MODEL), 0.1),
    }


def reference(x_prompt, x_sample, c_prompt, c_sample, state_conv, state_lru, state_pool, state_ffn_conv,
              w_ada, b_ada, g_pre1, w_in, w_conv, b_conv, w_a, b_a, w_i, b_i, lam,
              w_pool, pool_scale, w_out, g_post1, g_pre2, w_up, w_fconv, b_fconv, w_down, g_post2):
    yp, ys = x_prompt, x_sample
    conv_p, lru_p, pool_p, ffn_p = [], [], [], []
    conv_s, lru_s, pool_s, ffn_s = [], [], [], []
    dt = x_prompt.dtype
    for l in range(DEPTH):
        p = (w_ada[l], b_ada[l], g_pre1[l], w_in[l], w_conv[l], b_conv[l], w_a[l], b_a[l], w_i[l], b_i[l],
             lam[l], w_pool[l], pool_scale[l], w_out[l], g_post1[l], g_pre2[l], w_up[l], w_fconv[l],
             b_fconv[l], w_down[l], g_post2[l])
        yp, cp, hp, pp, fp = _layer(
            yp, c_prompt,
            jnp.zeros((BATCH, CONV_W - 1, LRU_W), dt),
            jnp.zeros((BATCH, LRU_W), dt),
            jnp.zeros((BATCH, POOL_BUF, POOL_W), dt),
            jnp.zeros((BATCH, FFN_CONV_W - 1, 2 * D_FF), dt),
            0, *p)
        ys, cs_, hs, ps, fs = _layer(
            ys, c_sample, state_conv[l], state_lru[l], state_pool[l], state_ffn_conv[l],
            PAST_LEN, *p)
        conv_p.append(cp); lru_p.append(hp); pool_p.append(pp); ffn_p.append(fp)
        conv_s.append(cs_); lru_s.append(hs); pool_s.append(ps); ffn_s.append(fs)
    return (yp, ys,
            jnp.stack(conv_p), jnp.stack(lru_p), jnp.stack(pool_p), jnp.stack(ffn_p),
            jnp.stack(conv_s), jnp.stack(lru_s), jnp.stack(pool_s), jnp.stack(ffn_s))
```

```python
import functools
import math

import jax
import jax.numpy as jnp
from jax import lax
from jax.experimental import pallas as pl
from jax.experimental.pallas import tpu as pltpu

D_MODEL = 1024
LRU_W = 512
LRU_HEADS = 8
LRU_HD = LRU_W // LRU_HEADS
LRU_C = 8.0
CONV_W = 4
POOL_W = 512
POOL_WINDOWS = (2, 4, 8, 16)
POOL_GW = POOL_W // len(POOL_WINDOWS)
POOL_BUF = max(POOL_WINDOWS) - 1
D_FF = 2816
FFN_CONV_W = 3
EPS = 1e-6
PAST_LEN = 16384

SUBLANES = 8
LANES = 128
MXU_DIM = 256
FFN_CHUNK = MXU_DIM
N_CHUNK = D_FF // FFN_CHUNK
HALF = MXU_DIM
VMEM_LIMIT_BYTES = 56 * 1024 * 1024

_BF16 = jnp.bfloat16
_F32 = jnp.float32


def _dot(a, b):
    return jnp.dot(a, b, preferred_element_type=_F32)


def _sigmoid(x):
    return 1.0 / (1.0 + jnp.exp(-x))


def _gelu(x):
    c = math.sqrt(2.0 / math.pi)
    return 0.5 * x * (1.0 + jnp.tanh(c * (x + 0.044715 * (x * x * x))))


def _rms(x, g):
    ms = jnp.mean(x * x, axis=-1, keepdims=True)
    return x * lax.rsqrt(ms + EPS) * g


def _prev_blocks(e):
    zero = jnp.zeros_like(e[:, :1])
    if e.shape[1] == 1:
        return zero
    return jnp.concatenate([zero, e[:, :-1]], axis=1)


def _shift_rows(e, prev, j, row):
    return pltpu.roll(jnp.where(row < SUBLANES - j, e, prev), j, 2)


def _ada_kernel(c_ref, w_ref, b_ref, o_ref):
    c = c_ref[...]
    a = (c * _sigmoid(c)).astype(_BF16)
    o_ref[...] = _dot(a, w_ref[...].astype(_BF16)) + b_ref[...]


def _ada_call(c_all, w_ada, b_ada):
    n, d = c_all.shape
    e = w_ada.shape[1]
    tn = 512
    return pl.pallas_call(
        _ada_kernel,
        grid=(e // tn,),
        in_specs=[
            pl.BlockSpec((n, d), lambda j: (0, 0)),
            pl.BlockSpec((d, tn), lambda j: (0, j)),
            pl.BlockSpec((1, tn), lambda j: (0, j)),
        ],
        out_specs=pl.BlockSpec((n, tn), lambda j: (0, j)),
        out_shape=jax.ShapeDtypeStruct((n, e), _F32),
        compiler_params=pltpu.CompilerParams(dimension_semantics=("arbitrary",)),
        name="adaln_mod",
    )(c_all, w_ada, b_ada.reshape(1, e))


def _layer_kernel(*refs, S, CB, prompt):
    if prompt:
        (x_ref, mod_ref,
         gpre1_ref, win_ref, wconv_ref, bconv_ref, wg_ref, ba_ref, bi_ref, lam_ref,
         wp_ref, pscale_ref, wout_ref, gpost1_ref, gpre2_ref,
         wup_ref, wfc_ref, bfc_ref, wdown_ref, gpost2_ref,
         y_ref, oconv_ref, olru_ref, opool_ref, offn_ref,
         hconv_ref, hlru_ref, hpool_ref, hffn_ref, mix_ref, acc_ref) = refs
    else:
        (x_ref, mod_ref, sconv_ref, slru_ref, spool_ref, sffn_ref,
         gpre1_ref, win_ref, wconv_ref, bconv_ref, wg_ref, ba_ref, bi_ref, lam_ref,
         wp_ref, pscale_ref, wout_ref, gpost1_ref, gpre2_ref,
         wup_ref, wfc_ref, bfc_ref, wdown_ref, gpost2_ref,
         y_ref, oconv_ref, olru_ref, opool_ref, offn_ref,
         mix_ref, acc_ref) = refs

    TM = S * CB * SUBLANES
    t = pl.program_id(1)

    if prompt:
        @pl.when(t == 0)
        def _():
            hconv_ref[...] = jnp.zeros_like(hconv_ref)
            hlru_ref[...] = jnp.zeros_like(hlru_ref)
            hpool_ref[...] = jnp.zeros_like(hpool_ref)
            hffn_ref[...] = jnp.zeros_like(hffn_ref)

    def blocks(v2, c):
        return v2.reshape(S, CB, SUBLANES, c)

    def flat(v4):
        return v4.reshape(-1, v4.shape[-1])

    def modv(k):
        return mod_ref[:, k:k + 1, :].reshape(S, 1, 1, D_MODEL)

    x4 = blocks(x_ref[...].reshape(TM, D_MODEL), D_MODEL)

    h = _rms(x4, gpre1_ref[...]) * (1.0 + modv(1)) + modv(0)
    u = _dot(flat(h).astype(_BF16), win_ref[...])
    xb = blocks(u[:, :LRU_W], LRU_W)
    gb = blocks(u[:, LRU_W:2 * LRU_W], LRU_W)
    pb = blocks(u[:, 2 * LRU_W:], POOL_W)

    hist_c = hconv_ref[...] if prompt else sconv_ref[...].reshape(S, 1, SUBLANES, LRU_W)
    e_c = jnp.concatenate([hist_c, xb], axis=1)
    p_c = _prev_blocks(e_c)
    row_c = lax.broadcasted_iota(jnp.int32, e_c.shape, 2)
    xc = bconv_ref[...] + wconv_ref[CONV_W - 1:CONV_W, :] * e_c
    for j in range(1, CONV_W):
        xc = xc + wconv_ref[CONV_W - 1 - j:CONV_W - j, :] * _shift_rows(e_c, p_c, j, row_c)
    xc = xc[:, 1:]
    if prompt:
        hconv_ref[...] = e_c[:, -1:]
    oconv_ref[...] = e_c[:, -1:].reshape(oconv_ref.shape)

    lam = lam_ref[...]
    softplus_neg_lam = jnp.maximum(-lam, 0.0) + jnp.log1p(jnp.exp(-jnp.abs(lam)))
    xc_bf = flat(xc).astype(_BF16)
    row_h = lax.broadcasted_iota(jnp.int32, (S, CB, SUBLANES, HALF), 2)
    for hh in range(LRU_W // HALF):
        sl = slice(hh * HALF, (hh + 1) * HALF)
        g = _dot(xc_bf[:, sl], wg_ref[hh])
        r = _sigmoid(blocks(g[:, :HALF], HALF) + ba_ref[:, sl])
        i = _sigmoid(blocks(g[:, HALF:], HALF) + bi_ref[:, sl])
        log_a = (-LRU_C * softplus_neg_lam[:, sl]) * r
        a = jnp.exp(log_a)
        th = jnp.tanh(log_a)
        b = jnp.sqrt(-2.0 * th / (1.0 - th)) * (i * xc[..., sl])
        for d in (1, 2, 4):
            a_sh = jnp.where(row_h >= d, pltpu.roll(a, d, 2), 1.0)
            b_sh = jnp.where(row_h >= d, pltpu.roll(b, d, 2), 0.0)
            b = a * b_sh + b
            a = a * a_sh
        if prompt:
            hcar = hlru_ref[:, sl]
            hs = []
            for k in range(CB):
                hk = a[0, k] * hcar + b[0, k]
                hs.append(hk)
                hcar = hk[SUBLANES - 1:SUBLANES, :]
            hr = jnp.stack(hs, axis=0).reshape(S, CB, SUBLANES, HALF)
            hlru_ref[:, sl] = hcar
            olru_ref[:, :, sl] = hcar.reshape(1, 1, HALF)
        else:
            h0 = slru_ref[:, :, sl].reshape(S, 1, 1, HALF)
            hr = a * h0 + b
            olru_ref[:, :, sl] = pltpu.roll(hr, 1, 2)[:, 0, 0:1, :]
        lru = hr * _gelu(gb[..., sl])
        mix_ref[:, sl] = flat(lru).astype(_BF16)

    hist_p = hpool_ref[...] if prompt else spool_ref[...].reshape(S, 2, SUBLANES, POOL_W)
    e_p = jnp.concatenate([hist_p, pb], axis=1)
    if prompt:
        hpool_ref[...] = e_p[:, -2:]
    opool_ref[...] = e_p[:, -2:].reshape(opool_ref.shape)
    pz_groups = []
    for gi, w in enumerate(POOL_WINDOWS):
        eg = e_p[..., gi * POOL_GW:(gi + 1) * POOL_GW]
        row_g = lax.broadcasted_iota(jnp.int32, eg.shape, 2)
        s = eg
        span = 1
        while span < w:
            if span < SUBLANES:
                s = s + _shift_rows(s, _prev_blocks(s), span, row_g)
            else:
                s = s + _prev_blocks(s)
            span *= 2
        s = s[:, 2:]
        cur = eg[:, 2:]
        if prompt:
            blk = lax.broadcasted_iota(jnp.int32, cur.shape, 1)
            pos = t * TM + blk * SUBLANES + row_g[:, 2:]
            inv = 1.0 / jnp.minimum(pos + 1, w).astype(_F32)
        else:
            inv = 1.0 / float(min(PAST_LEN + 1, w))
        pz_groups.append(s * inv - cur)
    for hh in range(POOL_W // HALF):
        sl = slice(hh * HALF, (hh + 1) * HALF)
        pz = jnp.concatenate(pz_groups[2 * hh:2 * hh + 2], axis=-1)
        q = _dot(flat(pz).astype(_BF16), wp_ref[hh]) * pscale_ref[:, sl]
        mix_ref[:, LRU_W + hh * HALF:LRU_W + (hh + 1) * HALF] = q.astype(_BF16)

    m = _dot(mix_ref[...], wout_ref[...])
    x1 = x4 + modv(2) * _rms(blocks(m, D_MODEL), gpost1_ref[...])

    h2 = (_rms(x1, gpre2_ref[...]) * (1.0 + modv(4)) + modv(3))
    h2_bf = flat(h2).astype(_BF16)
    acc_ref[...] = jnp.zeros_like(acc_ref)
    row_f = lax.broadcasted_iota(jnp.int32, (S, 1 + CB, SUBLANES, 2 * FFN_CHUNK), 2)

    def ffn_step(c, carry):
        up = _dot(h2_bf, wup_ref[c])
        hist_f = hffn_ref[c] if prompt else sffn_ref[c].reshape(S, 1, SUBLANES, 2 * FFN_CHUNK)
        e_f = jnp.concatenate([hist_f, blocks(up, 2 * FFN_CHUNK)], axis=1)
        p_f = _prev_blocks(e_f)
        wf = wfc_ref[c]
        upc = bfc_ref[c] + wf[2:3] * e_f
        upc = upc + wf[1:2] * _shift_rows(e_f, p_f, 1, row_f)
        upc = upc + wf[0:1] * _shift_rows(e_f, p_f, 2, row_f)
        upc = upc[:, 1:]
        f = _gelu(upc[..., :FFN_CHUNK]) * upc[..., FFN_CHUNK:]
        acc_ref[...] += _dot(flat(f).astype(_BF16), wdown_ref[c])
        if prompt:
            hffn_ref[c] = e_f[:, -1:]
        offn_ref[c] = e_f[:, -1:].reshape(offn_ref.shape[1:])
        return carry

    lax.fori_loop(0, N_CHUNK, ffn_step, 0)

    y = x1 + modv(5) * _rms(blocks(acc_ref[...], D_MODEL), gpost2_ref[...])
    y_ref[...] = y.reshape(y_ref.shape)


def _const_spec(shape):
    nd = len(shape)
    return pl.BlockSpec(shape, lambda i, t: (0,) * nd, pipeline_mode=pl.Buffered(1))


def _layer_call(x, mod, states, weights, *, prompt, seqs_per_tile, rows_per_tile):
    B, T, _ = x.shape
    S = seqs_per_tile
    CB = rows_per_tile // SUBLANES
    TM = S * rows_per_tile
    grid = (B // S, T // rows_per_tile)

    in_specs = [
        pl.BlockSpec((S, rows_per_tile, D_MODEL), lambda i, t: (i, t, 0)),
        pl.BlockSpec((S, 6, D_MODEL), lambda i, t: (i, 0, 0)),
    ]
    args = [x, mod]
    if not prompt:
        sconv, slru, spool, sffn = states
        in_specs += [
            pl.BlockSpec((S, SUBLANES, LRU_W), lambda i, t: (i, 0, 0)),
            pl.BlockSpec((S, 1, LRU_W), lambda i, t: (i, 0, 0)),
            pl.BlockSpec((S, 2 * SUBLANES, POOL_W), lambda i, t: (i, 0, 0)),
            pl.BlockSpec((N_CHUNK, S, SUBLANES, 2 * FFN_CHUNK), lambda i, t: (0, i, 0, 0)),
        ]
        args += [sconv, slru, spool, sffn]
    in_specs += [_const_spec(w.shape) for w in weights]
    args += list(weights)

    out_shape = (
        jax.ShapeDtypeStruct((B, T, D_MODEL), _F32),
        jax.ShapeDtypeStruct((B, SUBLANES, LRU_W), _F32),
        jax.ShapeDtypeStruct((B, 1, LRU_W), _F32),
        jax.ShapeDtypeStruct((B, 2 * SUBLANES, POOL_W), _F32),
        jax.ShapeDtypeStruct((N_CHUNK, B, SUBLANES, 2 * FFN_CHUNK), _F32),
    )
    out_specs = (
        pl.BlockSpec((S, rows_per_tile, D_MODEL), lambda i, t: (i, t, 0)),
        pl.BlockSpec((S, SUBLANES, LRU_W), lambda i, t: (i, 0, 0)),
        pl.BlockSpec((S, 1, LRU_W), lambda i, t: (i, 0, 0)),
        pl.BlockSpec((S, 2 * SUBLANES, POOL_W), lambda i, t: (i, 0, 0)),
        pl.BlockSpec((N_CHUNK, S, SUBLANES, 2 * FFN_CHUNK), lambda i, t: (0, i, 0, 0)),
    )
    scratch = []
    if prompt:
        scratch += [
            pltpu.VMEM((1, 1, SUBLANES, LRU_W), _F32),
            pltpu.VMEM((1, LRU_W), _F32),
            pltpu.VMEM((1, 2, SUBLANES, POOL_W), _F32),
            pltpu.VMEM((N_CHUNK, 1, 1, SUBLANES, 2 * FFN_CHUNK), _F32),
        ]
    scratch += [
        pltpu.VMEM((TM, LRU_W + POOL_W), _BF16),
        pltpu.VMEM((TM, D_MODEL), _F32),
    ]
    return pl.pallas_call(
        functools.partial(_layer_kernel, S=S, CB=CB, prompt=prompt),
        grid=grid,
        in_specs=in_specs,
        out_specs=out_specs,
        out_shape=out_shape,
        scratch_shapes=scratch,
        compiler_params=pltpu.CompilerParams(
            dimension_semantics=("arbitrary", "arbitrary"),
            vmem_limit_bytes=VMEM_LIMIT_BYTES),
        name="layer_prompt" if prompt else "layer_sample",
    )(*args)


def _block_diag(blocks):
    n, k, _ = blocks.shape
    eye = jnp.eye(n, dtype=blocks.dtype)
    return (eye[:, None, :, None] * blocks[:, :, None, :]).reshape(n * k, n * k)


def _chunk_cols(v):
    a = v[..., :D_FF].reshape(v.shape[:-1] + (N_CHUNK, FFN_CHUNK))
    g = v[..., D_FF:].reshape(v.shape[:-1] + (N_CHUNK, FFN_CHUNK))
    return jnp.moveaxis(jnp.concatenate([a, g], axis=-1), -2, 0)


def _unchunk_cols(v):
    v = jnp.moveaxis(v, 0, -2)
    a = v[..., :FFN_CHUNK].reshape(v.shape[:-2] + (D_FF,))
    g = v[..., FFN_CHUNK:].reshape(v.shape[:-2] + (D_FF,))
    return jnp.concatenate([a, g], axis=-1)


def kernel(x_prompt, x_sample, c_prompt, c_sample, state_conv, state_lru, state_pool, state_ffn_conv, w_ada, b_ada, g_pre1, w_in, w_conv, b_conv, w_a, b_a, w_i, b_i, lam, w_pool, pool_scale, w_out, g_post1, g_pre2, w_up, w_fconv, b_fconv, w_down, g_post2):
    l = 0
    nb_p = x_prompt.shape[0]
    nb_s = x_sample.shape[0]

    mod = _ada_call(jnp.concatenate([c_prompt, c_sample], axis=0), w_ada[l], b_ada[l])
    mod = mod.reshape(nb_p + nb_s, 6, D_MODEL)

    heads_per_half = HALF // LRU_HD
    groups_per_half = HALF // POOL_GW
    wg = jnp.stack([
        jnp.concatenate([
            _block_diag(w_a[l, hh * heads_per_half:(hh + 1) * heads_per_half]),
            _block_diag(w_i[l, hh * heads_per_half:(hh + 1) * heads_per_half])], axis=1)
        for hh in range(LRU_W // HALF)]).astype(_BF16)
    wp = jnp.stack([
        _block_diag(w_pool[l, hh * groups_per_half:(hh + 1) * groups_per_half])
        for hh in range(POOL_W // HALF)]).astype(_BF16)
    row = lambda v: v.reshape(1, -1)
    weights = (
        row(g_pre1[l]), w_in[l].astype(_BF16), w_conv[l], row(b_conv[l]),
        wg, row(b_a[l]), row(b_i[l]), row(lam[l]),
        wp, row(pool_scale[l]), w_out[l].astype(_BF16), row(g_post1[l]), row(g_pre2[l]),
        _chunk_cols(w_up[l]).astype(_BF16), _chunk_cols(w_fconv[l]), _chunk_cols(row(b_fconv[l])),
        w_down[l].reshape(N_CHUNK, FFN_CHUNK, D_MODEL).astype(_BF16), row(g_post2[l]),
    )

    yp, cp, hp, pp, fp = _layer_call(
        x_prompt, mod[:nb_p], None, weights,
        prompt=True, seqs_per_tile=1, rows_per_tile=256)

    sconv = jnp.pad(state_conv[l], ((0, 0), (SUBLANES - (CONV_W - 1), 0), (0, 0)))
    spool = jnp.pad(state_pool[l], ((0, 0), (2 * SUBLANES - POOL_BUF, 0), (0, 0)))
    sffn = _chunk_cols(jnp.pad(state_ffn_conv[l], ((0, 0), (SUBLANES - (FFN_CONV_W - 1), 0), (0, 0))))
    slru = state_lru[l].reshape(nb_s, 1, LRU_W)
    ys, cs, hs, ps, fs = _layer_call(
        x_sample, mod[nb_p:], (sconv, slru, spool, sffn), weights,
        prompt=False, seqs_per_tile=16, rows_per_tile=SUBLANES)

    def unpad(c, h, p, f):
        return (c[None, :, SUBLANES - (CONV_W - 1):],
                h.reshape(1, -1, LRU_W),
                p[None, :, 2 * SUBLANES - POOL_BUF:],
                _unchunk_cols(f)[None, :, SUBLANES - (FFN_CONV_W - 1):])

    cp, hp, pp, fp = unpad(cp, hp, pp, fp)
    cs, hs, ps, fs = unpad(cs, hs, ps, fs)
    return (yp, ys, cp, hp, pp, fp, cs, hs, ps, fs)
```

```python
import functools
import math

import jax
import jax.numpy as jnp
from jax import lax
from jax.experimental import pallas as pl
from jax.experimental.pallas import tpu as pltpu

D_MODEL = 1024
LRU_W = 512
LRU_HEADS = 8
LRU_HD = LRU_W // LRU_HEADS
LRU_C = 8.0
CONV_W = 4
POOL_W = 512
POOL_WINDOWS = (2, 4, 8, 16)
POOL_GW = POOL_W // len(POOL_WINDOWS)
POOL_BUF = max(POOL_WINDOWS) - 1
D_FF = 2816
FFN_CONV_W = 3
EPS = 1e-6
PAST_LEN = 16384

SUBLANES = 8
LANES = 128
MXU_DIM = 256
FFN_CHUNK = MXU_DIM
N_CHUNK = D_FF // FFN_CHUNK
HALF = MXU_DIM
VMEM_LIMIT_BYTES = 56 * 1024 * 1024

_BF16 = jnp.bfloat16
_F32 = jnp.float32


def _dot(a, b):
    return jnp.dot(a, b, preferred_element_type=_F32)


def _sigmoid(x):
    return 1.0 / (1.0 + jnp.exp(-x))


def _gelu_x2(x):
    c = math.sqrt(2.0 / math.pi)
    return x * (1.0 + jnp.tanh(x * (c + (c * 0.044715) * (x * x))))


def _rms(x, g):
    ms = jnp.mean(x * x, axis=-1, keepdims=True)
    return x * lax.rsqrt(ms + EPS) * g


def _prev_blocks(e):
    zero = jnp.zeros_like(e[:, :1])
    if e.shape[1] == 1:
        return zero
    return jnp.concatenate([zero, e[:, :-1]], axis=1)


def _shift_rows(e, prev, j, row):
    return pltpu.roll(jnp.where(row < SUBLANES - j, e, prev), j, 2)


def _ada_kernel(c_ref, w_ref, b_ref, o_ref):
    c = c_ref[...]
    a = (c * _sigmoid(c)).astype(_BF16)
    o_ref[...] = _dot(a, w_ref[...].astype(_BF16)) + b_ref[...]


def _ada_call(c_all, w_ada, b_ada):
    n, d = c_all.shape
    e = w_ada.shape[1]
    tn = 512
    return pl.pallas_call(
        _ada_kernel,
        grid=(e // tn,),
        in_specs=[
            pl.BlockSpec((n, d), lambda j: (0, 0)),
            pl.BlockSpec((d, tn), lambda j: (0, j)),
            pl.BlockSpec((1, tn), lambda j: (0, j)),
        ],
        out_specs=pl.BlockSpec((n, tn), lambda j: (0, j)),
        out_shape=jax.ShapeDtypeStruct((n, e), _F32),
        compiler_params=pltpu.CompilerParams(dimension_semantics=("arbitrary",)),
        name="adaln_mod",
    )(c_all, w_ada, b_ada.reshape(1, e))


def _layer_kernel(*refs, S, CB, prompt):
    if prompt:
        (x_ref, mod_ref,
         gpre1_ref, win_ref, wconv_ref, bconv_ref, wg_ref, ba_ref, bi_ref, lam_ref,
         wp_ref, pscale_ref, wout_ref, gpost1_ref, gpre2_ref,
         wup_ref, wfc_ref, bfc_ref, wdown_ref, gpost2_ref,
         y_ref, oconv_ref, olru_ref, opool_ref, offn_ref,
         hconv_ref, hlru_ref, hpool_ref, hffn_ref, mix_ref, acc_ref, h2_ref, up0_ref, up1_ref, f0_ref, f1_ref) = refs
    else:
        (x_ref, mod_ref, sconv_ref, slru_ref, spool_ref, sffn_ref,
         gpre1_ref, win_ref, wconv_ref, bconv_ref, wg_ref, ba_ref, bi_ref, lam_ref,
         wp_ref, pscale_ref, wout_ref, gpost1_ref, gpre2_ref,
         wup_ref, wfc_ref, bfc_ref, wdown_ref, gpost2_ref,
         y_ref, oconv_ref, olru_ref, opool_ref, offn_ref,
         mix_ref, acc_ref, h2_ref, up0_ref, up1_ref, f0_ref, f1_ref) = refs

    TM = S * CB * SUBLANES
    t = pl.program_id(1)

    if prompt:
        @pl.when(t == 0)
        def _():
            hconv_ref[...] = jnp.zeros_like(hconv_ref)
            hlru_ref[...] = jnp.zeros_like(hlru_ref)
            hpool_ref[...] = jnp.zeros_like(hpool_ref)
            hffn_ref[...] = jnp.zeros_like(hffn_ref)

    def blocks(v2, c):
        return v2.reshape(S, CB, SUBLANES, c)

    def flat(v4):
        return v4.reshape(-1, v4.shape[-1])

    def modv(k):
        return mod_ref[:, k:k + 1, :].reshape(S, 1, 1, D_MODEL)

    x4 = blocks(x_ref[...].reshape(TM, D_MODEL), D_MODEL)

    h = _rms(x4, gpre1_ref[...]) * (1.0 + modv(1)) + modv(0)
    u = _dot(flat(h).astype(_BF16), win_ref[...])
    xb = blocks(u[:, :LRU_W], LRU_W)
    gb = blocks(u[:, LRU_W:2 * LRU_W], LRU_W)
    pb = blocks(u[:, 2 * LRU_W:], POOL_W)

    hist_c = hconv_ref[...] if prompt else sconv_ref[...].reshape(S, 1, SUBLANES, LRU_W)
    p_c = hist_c if CB == 1 else jnp.concatenate([hist_c, xb[:, :-1]], axis=1)
    row_c = lax.broadcasted_iota(jnp.int32, xb.shape, 2)
    xc = bconv_ref[...] + wconv_ref[CONV_W - 1:CONV_W, :] * xb
    for j in range(1, CONV_W):
        xc = xc + wconv_ref[CONV_W - 1 - j:CONV_W - j, :] * _shift_rows(xb, p_c, j, row_c)
    if prompt:
        hconv_ref[...] = xb[:, -1:]
    oconv_ref[...] = xb[:, -1:].reshape(oconv_ref.shape)

    lam = lam_ref[...]
    softplus_neg_lam = jnp.maximum(-lam, 0.0) + jnp.log1p(jnp.exp(-jnp.abs(lam)))
    xc_bf = flat(xc).astype(_BF16)
    row_h = lax.broadcasted_iota(jnp.int32, (S, CB, SUBLANES, HALF), 2)
    for hh in range(LRU_W // HALF):
        sl = slice(hh * HALF, (hh + 1) * HALF)
        g = _dot(xc_bf[:, sl], wg_ref[hh])
        r = _sigmoid(blocks(g[:, :HALF], HALF) + ba_ref[:, sl])
        i = _sigmoid(blocks(g[:, HALF:], HALF) + bi_ref[:, sl])
        log_a = (-LRU_C * softplus_neg_lam[:, sl]) * r
        a = jnp.exp(log_a)
        th = jnp.tanh(log_a)
        m2 = -2.0 * th / (1.0 - th)
        mult = jnp.where(m2 == 0.0, 0.0, m2 * lax.rsqrt(m2))
        b = mult * (i * xc[..., sl])
        for d in (1, 2, 4):
            a_sh = jnp.where(row_h >= d, pltpu.roll(a, d, 2), 1.0)
            b_sh = jnp.where(row_h >= d, pltpu.roll(b, d, 2), 0.0)
            b = a * b_sh + b
            a = a * a_sh
        if prompt:
            hcar = hlru_ref[:, sl]
            hs = []
            for k in range(CB):
                hk = a[0, k] * hcar + b[0, k]
                hs.append(hk)
                hcar = hk[SUBLANES - 1:SUBLANES, :]
            hr = jnp.stack(hs, axis=0).reshape(S, CB, SUBLANES, HALF)
            hlru_ref[:, sl] = hcar
            olru_ref[:, :, sl] = hcar.reshape(1, 1, HALF)
        else:
            h0 = slru_ref[:, :, sl].reshape(S, 1, 1, HALF)
            hr = a * h0 + b
            olru_ref[:, :, sl] = pltpu.roll(hr, 1, 2)[:, 0, 0:1, :]
        lru = hr * _gelu_x2(gb[..., sl])
        mix_ref[:, sl] = flat(lru).astype(_BF16)

    hist_p = hpool_ref[...] if prompt else spool_ref[...].reshape(S, 2, SUBLANES, POOL_W)
    e_p = jnp.concatenate([hist_p, pb], axis=1)
    if prompt:
        hpool_ref[...] = e_p[:, -2:]
    opool_ref[...] = e_p[:, -2:].reshape(opool_ref.shape)
    pz_groups = []
    for gi, w in enumerate(POOL_WINDOWS):
        eg = e_p[..., gi * POOL_GW:(gi + 1) * POOL_GW]
        row_g = lax.broadcasted_iota(jnp.int32, eg.shape, 2)
        s = eg
        span = 1
        while span < w:
            if span < SUBLANES:
                s = s + _shift_rows(s, _prev_blocks(s), span, row_g)
            else:
                s = s + _prev_blocks(s)
            span *= 2
        s = s[:, 2:]
        cur = eg[:, 2:]
        if prompt:
            blk = lax.broadcasted_iota(jnp.int32, cur.shape, 1)
            pos = t * TM + blk * SUBLANES + row_g[:, 2:]
            inv = 1.0 / jnp.minimum(pos + 1, w).astype(_F32)
        else:
            inv = 1.0 / float(min(PAST_LEN + 1, w))
        pz_groups.append(s * inv - cur)
    for hh in range(POOL_W // HALF):
        sl = slice(hh * HALF, (hh + 1) * HALF)
        pz = jnp.concatenate(pz_groups[2 * hh:2 * hh + 2], axis=-1)
        q = _dot(flat(pz).astype(_BF16), wp_ref[hh]) * pscale_ref[:, sl]
        mix_ref[:, LRU_W + hh * HALF:LRU_W + (hh + 1) * HALF] = q.astype(_BF16)

    m = _dot(mix_ref[...], wout_ref[...])
    x1 = x4 + modv(2) * _rms(blocks(m, D_MODEL), gpost1_ref[...])

    h2 = (_rms(x1, gpre2_ref[...]) * (1.0 + modv(4)) + modv(3))
    h2_ref[...] = flat(h2).astype(_BF16)
    acc_ref[...] = jnp.zeros_like(acc_ref)
    row_f = lax.broadcasted_iota(jnp.int32, (S, CB, SUBLANES, 2 * FFN_CHUNK), 2)
    up_refs = (up0_ref, up1_ref)
    f_refs = (f0_ref, f1_ref)

    def stage_u(c, slot):
        up_refs[slot][...] = _dot(h2_ref[...], wup_ref[c])

    def stage_v(c, slot):
        cur = blocks(up_refs[slot][...], 2 * FFN_CHUNK)
        hist_f = hffn_ref[c] if prompt else sffn_ref[c].reshape(S, 1, SUBLANES, 2 * FFN_CHUNK)
        prev = hist_f if CB == 1 else jnp.concatenate([hist_f, cur[:, :-1]], axis=1)
        wf = wfc_ref[c]
        upc = bfc_ref[c] + wf[2:3] * cur
        upc = upc + wf[1:2] * _shift_rows(cur, prev, 1, row_f)
        upc = upc + wf[0:1] * _shift_rows(cur, prev, 2, row_f)
        f = _gelu_x2(upc[..., :FFN_CHUNK]) * upc[..., FFN_CHUNK:]
        f_refs[slot][...] = flat(f).astype(_BF16)
        if prompt:
            hffn_ref[c] = cur[:, -1:]
        offn_ref[c] = cur[:, -1:].reshape(offn_ref.shape[1:])

    def stage_d(c, slot):
        acc_ref[...] += _dot(f_refs[slot][...], wdown_ref[c])

    for k in range(N_CHUNK + 2):
        if 1 <= k <= N_CHUNK:
            stage_v(k - 1, (k - 1) % 2)
        if k >= 2:
            stage_d(k - 2, k % 2)
        if k < N_CHUNK:
            stage_u(k, k % 2)

    y = x1 + modv(5) * _rms(blocks(acc_ref[...], D_MODEL), gpost2_ref[...])
    y_ref[...] = y.reshape(y_ref.shape)


def _const_spec(shape):
    nd = len(shape)
    return pl.BlockSpec(shape, lambda i, t: (0,) * nd, pipeline_mode=pl.Buffered(1))


def _layer_call(x, mod, states, weights, *, prompt, seqs_per_tile, rows_per_tile):
    B, T, _ = x.shape
    S = seqs_per_tile
    CB = rows_per_tile // SUBLANES
    TM = S * rows_per_tile
    grid = (B // S, T // rows_per_tile)

    in_specs = [
        pl.BlockSpec((S, rows_per_tile, D_MODEL), lambda i, t: (i, t, 0)),
        pl.BlockSpec((S, 6, D_MODEL), lambda i, t: (i, 0, 0)),
    ]
    args = [x, mod]
    if not prompt:
        sconv, slru, spool, sffn = states
        in_specs += [
            pl.BlockSpec((S, SUBLANES, LRU_W), lambda i, t: (i, 0, 0)),
            pl.BlockSpec((S, 1, LRU_W), lambda i, t: (i, 0, 0)),
            pl.BlockSpec((S, 2 * SUBLANES, POOL_W), lambda i, t: (i, 0, 0)),
            pl.BlockSpec((N_CHUNK, S, SUBLANES, 2 * FFN_CHUNK), lambda i, t: (0, i, 0, 0)),
        ]
        args += [sconv, slru, spool, sffn]
    in_specs += [_const_spec(w.shape) for w in weights]
    args += list(weights)

    out_shape = (
        jax.ShapeDtypeStruct((B, T, D_MODEL), _F32),
        jax.ShapeDtypeStruct((B, SUBLANES, LRU_W), _F32),
        jax.ShapeDtypeStruct((B, 1, LRU_W), _F32),
        jax.ShapeDtypeStruct((B, 2 * SUBLANES, POOL_W), _F32),
        jax.ShapeDtypeStruct((N_CHUNK, B, SUBLANES, 2 * FFN_CHUNK), _F32),
    )
    out_specs = (
        pl.BlockSpec((S, rows_per_tile, D_MODEL), lambda i, t: (i, t, 0)),
        pl.BlockSpec((S, SUBLANES, LRU_W), lambda i, t: (i, 0, 0)),
        pl.BlockSpec((S, 1, LRU_W), lambda i, t: (i, 0, 0)),
        pl.BlockSpec((S, 2 * SUBLANES, POOL_W), lambda i, t: (i, 0, 0)),
        pl.BlockSpec((N_CHUNK, S, SUBLANES, 2 * FFN_CHUNK), lambda i, t: (0, i, 0, 0)),
    )
    scratch = []
    if prompt:
        scratch += [
            pltpu.VMEM((1, 1, SUBLANES, LRU_W), _F32),
            pltpu.VMEM((1, LRU_W), _F32),
            pltpu.VMEM((1, 2, SUBLANES, POOL_W), _F32),
            pltpu.VMEM((N_CHUNK, 1, 1, SUBLANES, 2 * FFN_CHUNK), _F32),
        ]
    scratch += [
        pltpu.VMEM((TM, LRU_W + POOL_W), _BF16),
        pltpu.VMEM((TM, D_MODEL), _F32),
        pltpu.VMEM((TM, D_MODEL), _BF16),
        pltpu.VMEM((TM, 2 * FFN_CHUNK), _F32),
        pltpu.VMEM((TM, 2 * FFN_CHUNK), _F32),
        pltpu.VMEM((TM, FFN_CHUNK), _BF16),
        pltpu.VMEM((TM, FFN_CHUNK), _BF16),
    ]
    return pl.pallas_call(
        functools.partial(_layer_kernel, S=S, CB=CB, prompt=prompt),
        grid=grid,
        in_specs=in_specs,
        out_specs=out_specs,
        out_shape=out_shape,
        scratch_shapes=scratch,
        compiler_params=pltpu.CompilerParams(
            dimension_semantics=("arbitrary", "arbitrary"),
            vmem_limit_bytes=VMEM_LIMIT_BYTES),
        name="layer_prompt" if prompt else "layer_sample",
    )(*args)


def _block_diag(blocks):
    n, k, _ = blocks.shape
    eye = jnp.eye(n, dtype=blocks.dtype)
    return (eye[:, None, :, None] * blocks[:, :, None, :]).reshape(n * k, n * k)


def _chunk_cols(v):
    a = v[..., :D_FF].reshape(v.shape[:-1] + (N_CHUNK, FFN_CHUNK))
    g = v[..., D_FF:].reshape(v.shape[:-1] + (N_CHUNK, FFN_CHUNK))
    return jnp.moveaxis(jnp.concatenate([a, g], axis=-1), -2, 0)


def _unchunk_cols(v):
    v = jnp.moveaxis(v, 0, -2)
    a = v[..., :FFN_CHUNK].reshape(v.shape[:-2] + (D_FF,))
    g = v[..., FFN_CHUNK:].reshape(v.shape[:-2] + (D_FF,))
    return jnp.concatenate([a, g], axis=-1)


def kernel(x_prompt, x_sample, c_prompt, c_sample, state_conv, state_lru, state_pool, state_ffn_conv, w_ada, b_ada, g_pre1, w_in, w_conv, b_conv, w_a, b_a, w_i, b_i, lam, w_pool, pool_scale, w_out, g_post1, g_pre2, w_up, w_fconv, b_fconv, w_down, g_post2):
    l = 0
    nb_p = x_prompt.shape[0]
    nb_s = x_sample.shape[0]

    mod = _ada_call(jnp.concatenate([c_prompt, c_sample], axis=0), w_ada[l], b_ada[l])
    mod = mod.reshape(nb_p + nb_s, 6, D_MODEL)

    heads_per_half = HALF // LRU_HD
    groups_per_half = HALF // POOL_GW
    wg = jnp.stack([
        jnp.concatenate([
            _block_diag(w_a[l, hh * heads_per_half:(hh + 1) * heads_per_half]),
            _block_diag(w_i[l, hh * heads_per_half:(hh + 1) * heads_per_half])], axis=1)
        for hh in range(LRU_W // HALF)]).astype(_BF16)
    wp = jnp.stack([
        _block_diag(w_pool[l, hh * groups_per_half:(hh + 1) * groups_per_half])
        for hh in range(POOL_W // HALF)]).astype(_BF16)
    gelu_rows = jnp.concatenate([jnp.full((LRU_W, 1), 0.5, _F32), jnp.ones((POOL_W, 1), _F32)])
    w_out_half = (w_out[l] * gelu_rows).astype(_BF16)
    row = lambda v: v.reshape(1, -1)
    weights = (
        row(g_pre1[l]), w_in[l].astype(_BF16), w_conv[l], row(b_conv[l]),
        wg, row(b_a[l]), row(b_i[l]), row(lam[l]),
        wp, row(pool_scale[l]), w_out_half, row(g_post1[l]), row(g_pre2[l]),
        _chunk_cols(w_up[l]).astype(_BF16), _chunk_cols(w_fconv[l]), _chunk_cols(row(b_fconv[l])),
        (0.5 * w_down[l]).reshape(N_CHUNK, FFN_CHUNK, D_MODEL).astype(_BF16), row(g_post2[l]),
    )

    yp, cp, hp, pp, fp = _layer_call(
        x_prompt, mod[:nb_p], None, weights,
        prompt=True, seqs_per_tile=1, rows_per_tile=256)

    sconv = jnp.pad(state_conv[l], ((0, 0), (SUBLANES - (CONV_W - 1), 0), (0, 0)))
    spool = jnp.pad(state_pool[l], ((0, 0), (2 * SUBLANES - POOL_BUF, 0), (0, 0)))
    sffn = _chunk_cols(jnp.pad(state_ffn_conv[l], ((0, 0), (SUBLANES - (FFN_CONV_W - 1), 0), (0, 0))))
    slru = state_lru[l].reshape(nb_s, 1, LRU_W)
    ys, cs, hs, ps, fs = _layer_call(
        x_sample, mod[nb_p:], (sconv, slru, spool, sffn), weights,
        prompt=False, seqs_per_tile=16, rows_per_tile=SUBLANES)

    def unpad(c, h, p, f):
        return (c[None, :, SUBLANES - (CONV_W - 1):],
                h.reshape(1, -1, LRU_W),
                p[None, :, 2 * SUBLANES - POOL_BUF:],
                _unchunk_cols(f)[None, :, SUBLANES - (FFN_CONV_W - 1):])

    cp, hp, pp, fp = unpad(cp, hp, pp, fp)
    cs, hs, ps, fs = unpad(cs, hs, ps, fs)
    return (yp, ys, cp, hp, pp, fp, cs, hs, ps, fs)
```

```python
import functools
import math

import jax
import jax.numpy as jnp
from jax import lax
from jax.experimental import pallas as pl
from jax.experimental.pallas import tpu as pltpu

D_MODEL = 1024
LRU_W = 512
LRU_HEADS = 8
LRU_HD = LRU_W // LRU_HEADS
LRU_C = 8.0
CONV_W = 4
POOL_W = 512
POOL_WINDOWS = (2, 4, 8, 16)
POOL_GW = POOL_W // len(POOL_WINDOWS)
POOL_BUF = max(POOL_WINDOWS) - 1
D_FF = 2816
FFN_CONV_W = 3
EPS = 1e-6
PAST_LEN = 16384

SUBLANES = 8
LANES = 128
MXU_DIM = 256
FFN_CHUNK = MXU_DIM
N_CHUNK = D_FF // FFN_CHUNK
HALF = MXU_DIM
VMEM_LIMIT_BYTES = 60 * 1024 * 1024

_BF16 = jnp.bfloat16
_F32 = jnp.float32


def _dot(a, b):
    return jnp.dot(a, b, preferred_element_type=_F32)


def _sigmoid(x):
    return 1.0 / (1.0 + jnp.exp(-x))


def _gelu_x2(x):
    c = math.sqrt(2.0 / math.pi)
    return x * (1.0 + jnp.tanh(x * (c + (c * 0.044715) * (x * x))))


def _rms(x, g):
    ms = jnp.mean(x * x, axis=-1, keepdims=True)
    return x * lax.rsqrt(ms + EPS) * g


def _prev_blocks(e):
    zero = jnp.zeros_like(e[:, :1])
    if e.shape[1] == 1:
        return zero
    return jnp.concatenate([zero, e[:, :-1]], axis=1)


def _shift_rows(e, prev, j, row):
    return pltpu.roll(jnp.where(row < SUBLANES - j, e, prev), j, 2)


def _ada_kernel(c_ref, w_ref, b_ref, o_ref):
    c = c_ref[...]
    a = (c * _sigmoid(c)).astype(_BF16)
    o_ref[...] = _dot(a, w_ref[...].astype(_BF16)) + b_ref[...]


def _ada_call(c_all, w_ada, b_ada):
    n, d = c_all.shape
    e = w_ada.shape[1]
    tn = 1024
    return pl.pallas_call(
        _ada_kernel,
        grid=(e // tn,),
        in_specs=[
            pl.BlockSpec((n, d), lambda j: (0, 0)),
            pl.BlockSpec((d, tn), lambda j: (0, j)),
            pl.BlockSpec((1, tn), lambda j: (0, j)),
        ],
        out_specs=pl.BlockSpec((n, tn), lambda j: (0, j)),
        out_shape=jax.ShapeDtypeStruct((n, e), _F32),
        compiler_params=pltpu.CompilerParams(dimension_semantics=("arbitrary",)),
        name="adaln_mod",
    )(c_all, w_ada, b_ada.reshape(1, e))


def _layer_kernel(*refs, S, CB, prompt):
    if prompt:
        (x_ref, mod_ref,
         gpre1_ref, win_ref, wconv_ref, bconv_ref, wg_ref, ba_ref, bi_ref, lam_ref,
         wp_ref, pscale_ref, wout_ref, gpost1_ref, gpre2_ref,
         wup_ref, wfc_ref, bfc_ref, wdown_ref, gpost2_ref,
         y_ref, oconv_ref, olru_ref, opool_ref, offn_ref,
         hconv_ref, hlru_ref, hpool_ref, hffn_ref, mix_ref, acc_ref, h2_ref, up0_ref, up1_ref, f0_ref, f1_ref) = refs
    else:
        (x_ref, mod_ref, sconv_ref, slru_ref, spool_ref, sffn_ref,
         gpre1_ref, win_ref, wconv_ref, bconv_ref, wg_ref, ba_ref, bi_ref, lam_ref,
         wp_ref, pscale_ref, wout_ref, gpost1_ref, gpre2_ref,
         wup_ref, wfc_ref, bfc_ref, wdown_ref, gpost2_ref,
         y_ref, oconv_ref, olru_ref, opool_ref, offn_ref,
         mix_ref, acc_ref, h2_ref, up0_ref, up1_ref, f0_ref, f1_ref) = refs

    TM = S * CB * SUBLANES
    t = pl.program_id(1)

    if prompt:
        @pl.when(t == 0)
        def _():
            hconv_ref[...] = jnp.zeros_like(hconv_ref)
            hlru_ref[...] = jnp.zeros_like(hlru_ref)
            hpool_ref[...] = jnp.zeros_like(hpool_ref)
            hffn_ref[...] = jnp.zeros_like(hffn_ref)

    def blocks(v2, c):
        return v2.reshape(S, CB, SUBLANES, c)

    def flat(v4):
        return v4.reshape(-1, v4.shape[-1])

    def modv(k):
        return mod_ref[:, k:k + 1, :].reshape(S, 1, 1, D_MODEL)

    x4 = blocks(x_ref[...].reshape(TM, D_MODEL), D_MODEL)

    h = _rms(x4, gpre1_ref[...]) * (1.0 + modv(1)) + modv(0)
    u = _dot(flat(h).astype(_BF16), win_ref[...])
    xb = blocks(u[:, :LRU_W], LRU_W)
    gb = blocks(u[:, LRU_W:2 * LRU_W], LRU_W)
    pb = blocks(u[:, 2 * LRU_W:], POOL_W)

    hist_c = hconv_ref[...] if prompt else sconv_ref[...].reshape(S, 1, SUBLANES, LRU_W)
    p_c = hist_c if CB == 1 else jnp.concatenate([hist_c, xb[:, :-1]], axis=1)
    row_c = lax.broadcasted_iota(jnp.int32, xb.shape, 2)
    xc = bconv_ref[...] + wconv_ref[CONV_W - 1:CONV_W, :] * xb
    for j in range(1, CONV_W):
        xc = xc + wconv_ref[CONV_W - 1 - j:CONV_W - j, :] * _shift_rows(xb, p_c, j, row_c)
    if prompt:
        hconv_ref[...] = xb[:, -1:]
    oconv_ref[...] = xb[:, -1:].reshape(oconv_ref.shape)

    lam = lam_ref[...]
    softplus_neg_lam = jnp.maximum(-lam, 0.0) + jnp.log1p(jnp.exp(-jnp.abs(lam)))
    xc_bf = flat(xc).astype(_BF16)
    row_h = lax.broadcasted_iota(jnp.int32, (S, CB, SUBLANES, HALF), 2)
    for hh in range(LRU_W // HALF):
        sl = slice(hh * HALF, (hh + 1) * HALF)
        g = _dot(xc_bf[:, sl], wg_ref[hh])
        r = _sigmoid(blocks(g[:, :HALF], HALF) + ba_ref[:, sl])
        i = _sigmoid(blocks(g[:, HALF:], HALF) + bi_ref[:, sl])
        log_a = (-LRU_C * softplus_neg_lam[:, sl]) * r
        a = jnp.exp(log_a)
        th = jnp.tanh(log_a)
        m2 = -2.0 * th / (1.0 - th)
        mult = jnp.where(m2 == 0.0, 0.0, m2 * lax.rsqrt(m2))
        b = mult * (i * xc[..., sl])
        for d in (1, 2, 4):
            a_sh = jnp.where(row_h >= d, pltpu.roll(a, d, 2), 1.0)
            b_sh = jnp.where(row_h >= d, pltpu.roll(b, d, 2), 0.0)
            b = a * b_sh + b
            a = a * a_sh
        if prompt:
            hcar = hlru_ref[:, sl]
            hs = []
            for k in range(CB):
                hk = a[0, k] * hcar + b[0, k]
                hs.append(hk)
                hcar = hk[SUBLANES - 1:SUBLANES, :]
            hr = jnp.stack(hs, axis=0).reshape(S, CB, SUBLANES, HALF)
            hlru_ref[:, sl] = hcar
            olru_ref[:, :, sl] = hcar.reshape(1, 1, HALF)
        else:
            h0 = slru_ref[:, :, sl].reshape(S, 1, 1, HALF)
            hr = a * h0 + b
            olru_ref[:, :, sl] = pltpu.roll(hr, 1, 2)[:, 0, 0:1, :]
        lru = hr * _gelu_x2(gb[..., sl])
        mix_ref[:, sl] = flat(lru).astype(_BF16)

    hist_p = hpool_ref[...] if prompt else spool_ref[...].reshape(S, 2, SUBLANES, POOL_W)
    e_p = jnp.concatenate([hist_p, pb], axis=1)
    if prompt:
        hpool_ref[...] = e_p[:, -2:]
    opool_ref[...] = e_p[:, -2:].reshape(opool_ref.shape)
    pz_groups = []
    for gi, w in enumerate(POOL_WINDOWS):
        eg = e_p[..., gi * POOL_GW:(gi + 1) * POOL_GW]
        row_g = lax.broadcasted_iota(jnp.int32, eg.shape, 2)
        s = eg
        span = 1
        while span < w:
            if span < SUBLANES:
                s = s + _shift_rows(s, _prev_blocks(s), span, row_g)
            else:
                s = s + _prev_blocks(s)
            span *= 2
        s = s[:, 2:]
        cur = eg[:, 2:]
        if prompt:
            blk = lax.broadcasted_iota(jnp.int32, cur.shape, 1)
            pos = t * TM + blk * SUBLANES + row_g[:, 2:]
            inv = 1.0 / jnp.minimum(pos + 1, w).astype(_F32)
        else:
            inv = 1.0 / float(min(PAST_LEN + 1, w))
        pz_groups.append(s * inv - cur)
    for hh in range(POOL_W // HALF):
        sl = slice(hh * HALF, (hh + 1) * HALF)
        pz = jnp.concatenate(pz_groups[2 * hh:2 * hh + 2], axis=-1)
        q = _dot(flat(pz).astype(_BF16), wp_ref[hh]) * pscale_ref[:, sl]
        mix_ref[:, LRU_W + hh * HALF:LRU_W + (hh + 1) * HALF] = q.astype(_BF16)

    m = _dot(mix_ref[...], wout_ref[...])
    x1 = x4 + modv(2) * _rms(blocks(m, D_MODEL), gpost1_ref[...])

    h2 = (_rms(x1, gpre2_ref[...]) * (1.0 + modv(4)) + modv(3))
    h2_ref[...] = flat(h2).astype(_BF16)
    acc_ref[...] = jnp.zeros_like(acc_ref)
    row_f = lax.broadcasted_iota(jnp.int32, (S, CB, SUBLANES, 2 * FFN_CHUNK), 2)
    up_refs = (up0_ref, up1_ref)
    f_refs = (f0_ref, f1_ref)

    def stage_u(c, slot):
        up_refs[slot][...] = _dot(h2_ref[...], wup_ref[c])

    def stage_v(c, slot):
        cur = blocks(up_refs[slot][...], 2 * FFN_CHUNK)
        act = slice(c * FFN_CHUNK, (c + 1) * FFN_CHUNK)
        gate = slice(D_FF + c * FFN_CHUNK, D_FF + (c + 1) * FFN_CHUNK)
        if prompt:
            hist_f = hffn_ref[c]
        else:
            hist_f = jnp.concatenate([sffn_ref[:, :, act], sffn_ref[:, :, gate]], axis=-1)
            hist_f = hist_f.reshape(S, 1, SUBLANES, 2 * FFN_CHUNK)
        prev = hist_f if CB == 1 else jnp.concatenate([hist_f, cur[:, :-1]], axis=1)
        wf = wfc_ref[c]
        upc = bfc_ref[c] + wf[2:3] * cur
        upc = upc + wf[1:2] * _shift_rows(cur, prev, 1, row_f)
        upc = upc + wf[0:1] * _shift_rows(cur, prev, 2, row_f)
        f = _gelu_x2(upc[..., :FFN_CHUNK]) * upc[..., FFN_CHUNK:]
        f_refs[slot][...] = flat(f).astype(_BF16)
        if prompt:
            hffn_ref[c] = cur[:, -1:]
        last = cur[:, -1:].reshape(S, SUBLANES, 2 * FFN_CHUNK)
        offn_ref[:, :, act] = last[..., :FFN_CHUNK]
        offn_ref[:, :, gate] = last[..., FFN_CHUNK:]

    def stage_d(c, slot):
        acc_ref[...] += _dot(f_refs[slot][...], wdown_ref[c])

    for k in range(N_CHUNK + 2):
        if 1 <= k <= N_CHUNK:
            stage_v(k - 1, (k - 1) % 2)
        if k >= 2:
            stage_d(k - 2, k % 2)
        if k < N_CHUNK:
            stage_u(k, k % 2)

    y = x1 + modv(5) * _rms(blocks(acc_ref[...], D_MODEL), gpost2_ref[...])
    y_ref[...] = y.reshape(y_ref.shape)


def _const_spec(shape):
    nd = len(shape)
    return pl.BlockSpec(shape, lambda i, t: (0,) * nd, pipeline_mode=pl.Buffered(1))


def _layer_call(x, mod, states, weights, *, prompt, seqs_per_tile, rows_per_tile):
    B, T, _ = x.shape
    S = seqs_per_tile
    CB = rows_per_tile // SUBLANES
    TM = S * rows_per_tile
    grid = (B // S, T // rows_per_tile)

    in_specs = [
        pl.BlockSpec((S, rows_per_tile, D_MODEL), lambda i, t: (i, t, 0)),
        pl.BlockSpec((S, 6, D_MODEL), lambda i, t: (i, 0, 0)),
    ]
    args = [x, mod]
    if not prompt:
        sconv, slru, spool, sffn = states
        in_specs += [
            pl.BlockSpec((S, SUBLANES, LRU_W), lambda i, t: (i, 0, 0)),
            pl.BlockSpec((S, 1, LRU_W), lambda i, t: (i, 0, 0)),
            pl.BlockSpec((S, 2 * SUBLANES, POOL_W), lambda i, t: (i, 0, 0)),
            pl.BlockSpec((S, SUBLANES, 2 * D_FF), lambda i, t: (i, 0, 0)),
        ]
        args += [sconv, slru, spool, sffn]
    in_specs += [_const_spec(w.shape) for w in weights]
    args += list(weights)

    out_shape = (
        jax.ShapeDtypeStruct((B, T, D_MODEL), _F32),
        jax.ShapeDtypeStruct((B, SUBLANES, LRU_W), _F32),
        jax.ShapeDtypeStruct((B, 1, LRU_W), _F32),
        jax.ShapeDtypeStruct((B, 2 * SUBLANES, POOL_W), _F32),
        jax.ShapeDtypeStruct((B, SUBLANES, 2 * D_FF), _F32),
    )
    out_specs = (
        pl.BlockSpec((S, rows_per_tile, D_MODEL), lambda i, t: (i, t, 0)),
        pl.BlockSpec((S, SUBLANES, LRU_W), lambda i, t: (i, 0, 0)),
        pl.BlockSpec((S, 1, LRU_W), lambda i, t: (i, 0, 0)),
        pl.BlockSpec((S, 2 * SUBLANES, POOL_W), lambda i, t: (i, 0, 0)),
        pl.BlockSpec((S, SUBLANES, 2 * D_FF), lambda i, t: (i, 0, 0)),
    )
    scratch = []
    if prompt:
        scratch += [
            pltpu.VMEM((1, 1, SUBLANES, LRU_W), _F32),
            pltpu.VMEM((1, LRU_W), _F32),
            pltpu.VMEM((1, 2, SUBLANES, POOL_W), _F32),
            pltpu.VMEM((N_CHUNK, 1, 1, SUBLANES, 2 * FFN_CHUNK), _F32),
        ]
    scratch += [
        pltpu.VMEM((TM, LRU_W + POOL_W), _BF16),
        pltpu.VMEM((TM, D_MODEL), _F32),
        pltpu.VMEM((TM, D_MODEL), _BF16),
        pltpu.VMEM((TM, 2 * FFN_CHUNK), _F32),
        pltpu.VMEM((TM, 2 * FFN_CHUNK), _F32),
        pltpu.VMEM((TM, FFN_CHUNK), _BF16),
        pltpu.VMEM((TM, FFN_CHUNK), _BF16),
    ]
    return pl.pallas_call(
        functools.partial(_layer_kernel, S=S, CB=CB, prompt=prompt),
        grid=grid,
        in_specs=in_specs,
        out_specs=out_specs,
        out_shape=out_shape,
        scratch_shapes=scratch,
        compiler_params=pltpu.CompilerParams(
            dimension_semantics=("arbitrary", "arbitrary"),
            vmem_limit_bytes=VMEM_LIMIT_BYTES),
        name="layer_prompt" if prompt else "layer_sample",
    )(*args)


def _block_diag(blocks):
    n, k, _ = blocks.shape
    eye = jnp.eye(n, dtype=blocks.dtype)
    return (eye[:, None, :, None] * blocks[:, :, None, :]).reshape(n * k, n * k)


def _chunk_cols(v):
    a = v[..., :D_FF].reshape(v.shape[:-1] + (N_CHUNK, FFN_CHUNK))
    g = v[..., D_FF:].reshape(v.shape[:-1] + (N_CHUNK, FFN_CHUNK))
    return jnp.moveaxis(jnp.concatenate([a, g], axis=-1), -2, 0)


def kernel(x_prompt, x_sample, c_prompt, c_sample, state_conv, state_lru, state_pool, state_ffn_conv, w_ada, b_ada, g_pre1, w_in, w_conv, b_conv, w_a, b_a, w_i, b_i, lam, w_pool, pool_scale, w_out, g_post1, g_pre2, w_up, w_fconv, b_fconv, w_down, g_post2):
    l = 0
    nb_p = x_prompt.shape[0]
    nb_s = x_sample.shape[0]

    mod = _ada_call(jnp.concatenate([c_prompt, c_sample], axis=0), w_ada[l], b_ada[l])
    mod = mod.reshape(nb_p + nb_s, 6, D_MODEL)

    heads_per_half = HALF // LRU_HD
    groups_per_half = HALF // POOL_GW
    wg = jnp.stack([
        jnp.concatenate([
            _block_diag(w_a[l, hh * heads_per_half:(hh + 1) * heads_per_half]),
            _block_diag(w_i[l, hh * heads_per_half:(hh + 1) * heads_per_half])], axis=1)
        for hh in range(LRU_W // HALF)]).astype(_BF16)
    wp = jnp.stack([
        _block_diag(w_pool[l, hh * groups_per_half:(hh + 1) * groups_per_half])
        for hh in range(POOL_W // HALF)]).astype(_BF16)
    gelu_rows = jnp.concatenate([jnp.full((LRU_W, 1), 0.5, _F32), jnp.ones((POOL_W, 1), _F32)])
    w_out_half = (w_out[l] * gelu_rows).astype(_BF16)
    row = lambda v: v.reshape(1, -1)
    weights = (
        row(g_pre1[l]), w_in[l].astype(_BF16), w_conv[l], row(b_conv[l]),
        wg, row(b_a[l]), row(b_i[l]), row(lam[l]),
        wp, row(pool_scale[l]), w_out_half, row(g_post1[l]), row(g_pre2[l]),
        _chunk_cols(w_up[l]).astype(_BF16), _chunk_cols(w_fconv[l]), _chunk_cols(row(b_fconv[l])),
        (0.5 * w_down[l]).reshape(N_CHUNK, FFN_CHUNK, D_MODEL).astype(_BF16), row(g_post2[l]),
    )

    yp, cp, hp, pp, fp = _layer_call(
        x_prompt, mod[:nb_p], None, weights,
        prompt=True, seqs_per_tile=1, rows_per_tile=256)

    sconv = jnp.pad(state_conv[l], ((0, 0), (SUBLANES - (CONV_W - 1), 0), (0, 0)))
    spool = jnp.pad(state_pool[l], ((0, 0), (2 * SUBLANES - POOL_BUF, 0), (0, 0)))
    sffn = jnp.pad(state_ffn_conv[l], ((0, 0), (SUBLANES - (FFN_CONV_W - 1), 0), (0, 0)))
    slru = state_lru[l].reshape(nb_s, 1, LRU_W)
    ys, cs, hs, ps, fs = _layer_call(
        x_sample, mod[nb_p:], (sconv, slru, spool, sffn), weights,
        prompt=False, seqs_per_tile=32, rows_per_tile=SUBLANES)

    def unpad(c, h, p, f):
        return (c[None, :, SUBLANES - (CONV_W - 1):],
                h.reshape(1, -1, LRU_W),
                p[None, :, 2 * SUBLANES - POOL_BUF:],
                f[None, :, SUBLANES - (FFN_CONV_W - 1):])

    cp, hp, pp, fp = unpad(cp, hp, pp, fp)
    cs, hs, ps, fs = unpad(cs, hs, ps, fs)
    return (yp, ys, cp, hp, pp, fp, cs, hs, ps, fs)
```

```python
import functools
import math

import jax
import jax.numpy as jnp
from jax import lax
from jax.experimental import pallas as pl
from jax.experimental.pallas import tpu as pltpu

D_MODEL = 1024
LRU_W = 512
LRU_HEADS = 8
LRU_HD = LRU_W // LRU_HEADS
LRU_C = 8.0
CONV_W = 4
POOL_W = 512
POOL_WINDOWS = (2, 4, 8, 16)
POOL_GW = POOL_W // len(POOL_WINDOWS)
POOL_BUF = max(POOL_WINDOWS) - 1
D_FF = 2816
FFN_CONV_W = 3
EPS = 1e-6
PAST_LEN = 16384

SUBLANES = 8
LANES = 128
MXU_DIM = 256
FFN_CHUNK = MXU_DIM
N_CHUNK = D_FF // FFN_CHUNK
HALF = MXU_DIM
VMEM_LIMIT_BYTES = 60 * 1024 * 1024

_BF16 = jnp.bfloat16
_F32 = jnp.float32


def _dot(a, b):
    return jnp.dot(a, b, preferred_element_type=_F32)


def _sigmoid(x):
    return 1.0 / (1.0 + jnp.exp(-x))


def _gelu_x2(x):
    c = math.sqrt(2.0 / math.pi)
    return x * (1.0 + jnp.tanh(x * (c + (c * 0.044715) * (x * x))))


def _rms(x, g):
    ms = jnp.mean(x * x, axis=-1, keepdims=True)
    return x * lax.rsqrt(ms + EPS) * g


def _prev_blocks(e):
    zero = jnp.zeros_like(e[:, :1])
    if e.shape[1] == 1:
        return zero
    return jnp.concatenate([zero, e[:, :-1]], axis=1)


def _shift_rows(e, prev, j, row):
    return pltpu.roll(jnp.where(row < SUBLANES - j, e, prev), j, 2)


def _ada_kernel(c_ref, w_ref, b_ref, o_ref):
    c = c_ref[...]
    a = (c * _sigmoid(c)).astype(_BF16)
    o_ref[...] = _dot(a, w_ref[...].astype(_BF16)) + b_ref[...]


def _ada_call(c_all, w_ada, b_ada):
    n, d = c_all.shape
    e = w_ada.shape[1]
    tn = 1024
    return pl.pallas_call(
        _ada_kernel,
        grid=(e // tn,),
        in_specs=[
            pl.BlockSpec((n, d), lambda j: (0, 0)),
            pl.BlockSpec((d, tn), lambda j: (0, j)),
            pl.BlockSpec((1, tn), lambda j: (0, j)),
        ],
        out_specs=pl.BlockSpec((n, tn), lambda j: (0, j)),
        out_shape=jax.ShapeDtypeStruct((n, e), _F32),
        compiler_params=pltpu.CompilerParams(dimension_semantics=("arbitrary",)),
        name="adaln_mod",
    )(c_all, w_ada, b_ada.reshape(1, e))


def _layer_kernel(*refs, S, CB, prompt, n_tiles, tiles_per_seq):
    if prompt:
        (x_ref, moda_ref, modb_ref,
         gpre1_ref, win_ref, wconv_ref, bconv_ref, wg_ref, ba_ref, bi_ref, lam_ref,
         wp_ref, pscale_ref, wout_ref, gpost1_ref, gpre2_ref,
         wup_ref, wfc_ref, bfc_ref, wdown_ref, gpost2_ref,
         y_ref, oconv_ref, olru_ref, opool_ref, offn_ref,
         hconv_ref, hlru_ref, hpool_ref, hffn_ref,
         mix_ref, x1_ref, h2_ref, acc_ref, up0_ref, up1_ref, f0_ref, f1_ref) = refs
    else:
        (x_ref, moda_ref, sconv_ref, slru_ref, spool_ref, sffn_ref,
         gpre1_ref, win_ref, wconv_ref, bconv_ref, wg_ref, ba_ref, bi_ref, lam_ref,
         wp_ref, pscale_ref, wout_ref, gpost1_ref, gpre2_ref,
         wup_ref, wfc_ref, bfc_ref, wdown_ref, gpost2_ref,
         y_ref, oconv_ref, olru_ref, opool_ref, offn_ref,
         mix_ref, x1_ref, h2_ref, acc_ref, up0_ref, up1_ref, f0_ref, f1_ref) = refs
        modb_ref = moda_ref

    TM = S * CB * SUBLANES
    s = pl.program_id(0)

    if prompt:
        t_a = lax.rem(jnp.minimum(s, n_tiles - 1), tiles_per_seq)
        ffn_seq_start = lax.rem(s + tiles_per_seq - 1, tiles_per_seq) == 0

        @pl.when(s == 0)
        def _():
            x1_ref[...] = jnp.zeros_like(x1_ref)
            h2_ref[...] = jnp.zeros_like(h2_ref)

        @pl.when(lax.rem(s, tiles_per_seq) == 0)
        def _():
            hconv_ref[...] = jnp.zeros_like(hconv_ref)
            hlru_ref[...] = jnp.zeros_like(hlru_ref)
            hpool_ref[...] = jnp.zeros_like(hpool_ref)

        @pl.when(jnp.logical_or(s == 0, ffn_seq_start))
        def _():
            hffn_ref[...] = jnp.zeros_like(hffn_ref)

    def blocks(v2, c):
        return v2.reshape(S, CB, SUBLANES, c)

    def flat(v4):
        return v4.reshape(-1, v4.shape[-1])

    def modv(ref, k):
        return ref[:, k:k + 1, :].reshape(S, 1, 1, D_MODEL)

    def token_mixing(out):
        x4 = blocks(x_ref[...].reshape(TM, D_MODEL), D_MODEL)
        h = _rms(x4, gpre1_ref[...]) * (1.0 + modv(moda_ref, 1)) + modv(moda_ref, 0)
        h_bf = flat(h).astype(_BF16)
        yield
        u = _dot(h_bf, win_ref[...])
        xb = blocks(u[:, :LRU_W], LRU_W)
        gb = blocks(u[:, LRU_W:2 * LRU_W], LRU_W)
        pb = blocks(u[:, 2 * LRU_W:], POOL_W)

        hist_c = hconv_ref[...] if prompt else sconv_ref[...].reshape(S, 1, SUBLANES, LRU_W)
        p_c = hist_c if CB == 1 else jnp.concatenate([hist_c, xb[:, :-1]], axis=1)
        row_c = lax.broadcasted_iota(jnp.int32, xb.shape, 2)
        xc = bconv_ref[...] + wconv_ref[CONV_W - 1:CONV_W, :] * xb
        for j in range(1, CONV_W):
            xc = xc + wconv_ref[CONV_W - 1 - j:CONV_W - j, :] * _shift_rows(xb, p_c, j, row_c)
        if prompt:
            hconv_ref[...] = xb[:, -1:]
        new_conv = xb[:, -1:].reshape(oconv_ref.shape)

        lam = lam_ref[...]
        softplus_neg_lam = jnp.maximum(-lam, 0.0) + jnp.log1p(jnp.exp(-jnp.abs(lam)))
        xc_bf = flat(xc).astype(_BF16)
        row_h = lax.broadcasted_iota(jnp.int32, (S, CB, SUBLANES, HALF), 2)
        new_lru = []
        for hh in range(LRU_W // HALF):
            yield
            sl = slice(hh * HALF, (hh + 1) * HALF)
            g = _dot(xc_bf[:, sl], wg_ref[hh])
            r = _sigmoid(blocks(g[:, :HALF], HALF) + ba_ref[:, sl])
            i = _sigmoid(blocks(g[:, HALF:], HALF) + bi_ref[:, sl])
            log_a = (-LRU_C * softplus_neg_lam[:, sl]) * r
            a = jnp.exp(log_a)
            th = jnp.tanh(log_a)
            m2 = -2.0 * th / (1.0 - th)
            mult = jnp.where(m2 == 0.0, 0.0, m2 * lax.rsqrt(m2))
            b = mult * (i * xc[..., sl])
            yield
            for d in (1, 2, 4):
                a_sh = jnp.where(row_h >= d, pltpu.roll(a, d, 2), 1.0)
                b_sh = jnp.where(row_h >= d, pltpu.roll(b, d, 2), 0.0)
                b = a * b_sh + b
                a = a * a_sh
            if prompt:
                hcar = hlru_ref[:, sl]
                hs = []
                for k in range(CB):
                    hk = a[0, k] * hcar + b[0, k]
                    hs.append(hk)
                    hcar = hk[SUBLANES - 1:SUBLANES, :]
                hr = jnp.stack(hs, axis=0).reshape(S, CB, SUBLANES, HALF)
                hlru_ref[:, sl] = hcar
                new_lru.append(hcar.reshape(1, 1, HALF))
            else:
                h0 = slru_ref[:, :, sl].reshape(S, 1, 1, HALF)
                hr = a * h0 + b
                new_lru.append(pltpu.roll(hr, 1, 2)[:, 0, 0:1, :])
            lru = hr * _gelu_x2(gb[..., sl])
            mix_ref[:, sl] = flat(lru).astype(_BF16)

        yield
        hist_p = hpool_ref[...] if prompt else spool_ref[...].reshape(S, 2, SUBLANES, POOL_W)
        e_p = jnp.concatenate([hist_p, pb], axis=1)
        if prompt:
            hpool_ref[...] = e_p[:, -2:]
        new_pool = e_p[:, -2:].reshape(opool_ref.shape)
        pz_groups = []
        for gi, w in enumerate(POOL_WINDOWS):
            eg = e_p[..., gi * POOL_GW:(gi + 1) * POOL_GW]
            row_g = lax.broadcasted_iota(jnp.int32, eg.shape, 2)
            acc = eg
            span = 1
            while span < w:
                if span < SUBLANES:
                    acc = acc + _shift_rows(acc, _prev_blocks(acc), span, row_g)
                else:
                    acc = acc + _prev_blocks(acc)
                span *= 2
            acc = acc[:, 2:]
            cur = eg[:, 2:]
            if prompt:
                blk = lax.broadcasted_iota(jnp.int32, cur.shape, 1)
                pos = t_a * TM + blk * SUBLANES + row_g[:, 2:]
                inv = 1.0 / jnp.minimum(pos + 1, w).astype(_F32)
            else:
                inv = 1.0 / float(min(PAST_LEN + 1, w))
            pz_groups.append(acc * inv - cur)
        for hh in range(POOL_W // HALF):
            sl = slice(hh * HALF, (hh + 1) * HALF)
            pz = jnp.concatenate(pz_groups[2 * hh:2 * hh + 2], axis=-1)
            q = _dot(flat(pz).astype(_BF16), wp_ref[hh]) * pscale_ref[:, sl]
            mix_ref[:, LRU_W + hh * HALF:LRU_W + (hh + 1) * HALF] = q.astype(_BF16)

        yield
        m = _dot(mix_ref[...], wout_ref[...])
        x1 = x4 + modv(moda_ref, 2) * _rms(blocks(m, D_MODEL), gpost1_ref[...])
        yield
        h2 = _rms(x1, gpre2_ref[...]) * (1.0 + modv(moda_ref, 4)) + modv(moda_ref, 3)
        out.update(x1=x1, h2=h2, conv=new_conv, lru=jnp.concatenate(new_lru, axis=-1),
                   pool=new_pool)

    def conv_ffn(other_work=iter(())):
        acc_ref[...] = jnp.zeros_like(acc_ref)
        row_f = lax.broadcasted_iota(jnp.int32, (S, CB, SUBLANES, 2 * FFN_CHUNK), 2)
        up_refs = (up0_ref, up1_ref)
        f_refs = (f0_ref, f1_ref)

        def stage_u(c, slot):
            cols = slice(c * 2 * FFN_CHUNK, (c + 1) * 2 * FFN_CHUNK)
            up_refs[slot][...] = _dot(h2_ref[...], wup_ref[:, cols])

        def stage_v(c, slot):
            cur = blocks(up_refs[slot][...], 2 * FFN_CHUNK)
            cols = slice(c * 2 * FFN_CHUNK, (c + 1) * 2 * FFN_CHUNK)
            act = slice(c * FFN_CHUNK, (c + 1) * FFN_CHUNK)
            gate = slice(D_FF + c * FFN_CHUNK, D_FF + (c + 1) * FFN_CHUNK)
            if prompt:
                hist_f = hffn_ref[c]
            else:
                hist_f = jnp.concatenate([sffn_ref[:, :, act], sffn_ref[:, :, gate]], axis=-1)
                hist_f = hist_f.reshape(S, 1, SUBLANES, 2 * FFN_CHUNK)
            prev = hist_f if CB == 1 else jnp.concatenate([hist_f, cur[:, :-1]], axis=1)
            upc = bfc_ref[:, cols] + wfc_ref[2:3, cols] * cur
            upc = upc + wfc_ref[1:2, cols] * _shift_rows(cur, prev, 1, row_f)
            upc = upc + wfc_ref[0:1, cols] * _shift_rows(cur, prev, 2, row_f)
            f = _gelu_x2(upc[..., :FFN_CHUNK]) * upc[..., FFN_CHUNK:]
            f_refs[slot][...] = flat(f).astype(_BF16)
            if prompt:
                hffn_ref[c] = cur[:, -1:]
            last = cur[:, -1:].reshape(S, SUBLANES, 2 * FFN_CHUNK)
            offn_ref[:, :, act] = last[..., :FFN_CHUNK]
            offn_ref[:, :, gate] = last[..., FFN_CHUNK:]

        def stage_d(c, slot):
            rows = slice(c * FFN_CHUNK, (c + 1) * FFN_CHUNK)
            acc_ref[...] += _dot(f_refs[slot][...], wdown_ref[rows, :])

        for k in range(N_CHUNK + 2):
            if 1 <= k <= N_CHUNK:
                stage_v(k - 1, (k - 1) % 2)
            if k >= 2:
                stage_d(k - 2, k % 2)
            if k < N_CHUNK:
                stage_u(k, k % 2)
            next(other_work, None)
        for _ in other_work:
            pass

        x1 = blocks(x1_ref[...], D_MODEL)
        y = x1 + modv(modb_ref, 5) * _rms(blocks(acc_ref[...], D_MODEL), gpost2_ref[...])
        y_ref[...] = y.reshape(y_ref.shape)

    out = {}
    if prompt:
        conv_ffn(token_mixing(out))
        x1_ref[...] = flat(out["x1"])
        h2_ref[...] = flat(out["h2"]).astype(_BF16)

        @pl.when(s < n_tiles)
        def _():
            oconv_ref[...] = out["conv"]
            olru_ref[...] = out["lru"]
            opool_ref[...] = out["pool"]
    else:
        for _ in token_mixing(out):
            pass
        x1_ref[...] = flat(out["x1"])
        h2_ref[...] = flat(out["h2"]).astype(_BF16)
        oconv_ref[...] = out["conv"]
        olru_ref[...] = out["lru"]
        opool_ref[...] = out["pool"]
        conv_ffn()


def _const_spec(shape):
    nd = len(shape)
    return pl.BlockSpec(shape, lambda s: (0,) * nd, pipeline_mode=pl.Buffered(1))


def _layer_call(x, mod, states, weights, *, prompt, seqs_per_tile, rows_per_tile):
    B, T, _ = x.shape
    S = seqs_per_tile
    CB = rows_per_tile // SUBLANES
    TM = S * rows_per_tile
    tiles_per_seq = T // rows_per_tile
    n_tiles = (B // S) * tiles_per_seq

    if prompt:
        assert S == 1
        grid = (n_tiles + 1,)

        def tile_a(s):
            return jnp.minimum(s, n_tiles - 1)

        def tile_b(s):
            return jnp.maximum(s - 1, 0)

        x_map = lambda s: (tile_a(s) // tiles_per_seq, tile_a(s) % tiles_per_seq, 0)
        y_map = lambda s: (tile_b(s) // tiles_per_seq, tile_b(s) % tiles_per_seq, 0)
        seq_a = lambda s: (tile_a(s) // tiles_per_seq, 0, 0)
        seq_b = lambda s: (tile_b(s) // tiles_per_seq, 0, 0)
    else:
        assert tiles_per_seq == 1
        grid = (n_tiles,)
        x_map = y_map = seq_a = seq_b = lambda s: (s, 0, 0)

    in_specs = [
        pl.BlockSpec((S, rows_per_tile, D_MODEL), x_map),
        pl.BlockSpec((S, 6, D_MODEL), seq_a),
    ]
    args = [x, mod]
    if prompt:
        in_specs.append(pl.BlockSpec((S, 6, D_MODEL), seq_b))
        args.append(mod)
    else:
        sconv, slru, spool, sffn = states
        in_specs += [
            pl.BlockSpec((S, SUBLANES, LRU_W), seq_a),
            pl.BlockSpec((S, 1, LRU_W), seq_a),
            pl.BlockSpec((S, 2 * SUBLANES, POOL_W), seq_a),
            pl.BlockSpec((S, SUBLANES, 2 * D_FF), seq_a),
        ]
        args += [sconv, slru, spool, sffn]
    in_specs += [_const_spec(w.shape) for w in weights]
    args += list(weights)

    out_shape = (
        jax.ShapeDtypeStruct((B, T, D_MODEL), _F32),
        jax.ShapeDtypeStruct((B, SUBLANES, LRU_W), _F32),
        jax.ShapeDtypeStruct((B, 1, LRU_W), _F32),
        jax.ShapeDtypeStruct((B, 2 * SUBLANES, POOL_W), _F32),
        jax.ShapeDtypeStruct((B, SUBLANES, 2 * D_FF), _F32),
    )
    out_specs = (
        pl.BlockSpec((S, rows_per_tile, D_MODEL), y_map),
        pl.BlockSpec((S, SUBLANES, LRU_W), seq_a),
        pl.BlockSpec((S, 1, LRU_W), seq_a),
        pl.BlockSpec((S, 2 * SUBLANES, POOL_W), seq_a),
        pl.BlockSpec((S, SUBLANES, 2 * D_FF), seq_b),
    )
    scratch = []
    if prompt:
        scratch += [
            pltpu.VMEM((1, 1, SUBLANES, LRU_W), _F32),
            pltpu.VMEM((1, LRU_W), _F32),
            pltpu.VMEM((1, 2, SUBLANES, POOL_W), _F32),
            pltpu.VMEM((N_CHUNK, 1, 1, SUBLANES, 2 * FFN_CHUNK), _F32),
        ]
    scratch += [
        pltpu.VMEM((TM, LRU_W + POOL_W), _BF16),
        pltpu.VMEM((TM, D_MODEL), _F32),
        pltpu.VMEM((TM, D_MODEL), _BF16),
        pltpu.VMEM((TM, D_MODEL), _F32),
        pltpu.VMEM((TM, 2 * FFN_CHUNK), _F32),
        pltpu.VMEM((TM, 2 * FFN_CHUNK), _F32),
        pltpu.VMEM((TM, FFN_CHUNK), _BF16),
        pltpu.VMEM((TM, FFN_CHUNK), _BF16),
    ]
    return pl.pallas_call(
        functools.partial(_layer_kernel, S=S, CB=CB, prompt=prompt,
                          n_tiles=n_tiles, tiles_per_seq=tiles_per_seq),
        grid=grid,
        in_specs=in_specs,
        out_specs=out_specs,
        out_shape=out_shape,
        scratch_shapes=scratch,
        compiler_params=pltpu.CompilerParams(
            dimension_semantics=("arbitrary",),
            vmem_limit_bytes=VMEM_LIMIT_BYTES),
        name="layer_prompt" if prompt else "layer_sample",
    )(*args)


def _block_diag(blocks):
    n, k, _ = blocks.shape
    eye = jnp.eye(n, dtype=blocks.dtype)
    return (eye[:, None, :, None] * blocks[:, :, None, :]).reshape(n * k, n * k)


def _chunk_cols(v):
    a = v[..., :D_FF].reshape(v.shape[:-1] + (N_CHUNK, FFN_CHUNK))
    g = v[..., D_FF:].reshape(v.shape[:-1] + (N_CHUNK, FFN_CHUNK))
    return jnp.concatenate([a, g], axis=-1).reshape(v.shape)


def kernel(x_prompt, x_sample, c_prompt, c_sample, state_conv, state_lru, state_pool, state_ffn_conv, w_ada, b_ada, g_pre1, w_in, w_conv, b_conv, w_a, b_a, w_i, b_i, lam, w_pool, pool_scale, w_out, g_post1, g_pre2, w_up, w_fconv, b_fconv, w_down, g_post2):
    l = 0
    nb_p = x_prompt.shape[0]
    nb_s = x_sample.shape[0]

    mod = _ada_call(jnp.concatenate([c_prompt, c_sample], axis=0), w_ada[l], b_ada[l])
    mod = mod.reshape(nb_p + nb_s, 6, D_MODEL)

    heads_per_half = HALF // LRU_HD
    groups_per_half = HALF // POOL_GW
    wg = jnp.stack([
        jnp.concatenate([
            _block_diag(w_a[l, hh * heads_per_half:(hh + 1) * heads_per_half]),
            _block_diag(w_i[l, hh * heads_per_half:(hh + 1) * heads_per_half])], axis=1)
        for hh in range(LRU_W // HALF)]).astype(_BF16)
    wp = jnp.stack([
        _block_diag(w_pool[l, hh * groups_per_half:(hh + 1) * groups_per_half])
        for hh in range(POOL_W // HALF)]).astype(_BF16)
    gelu_rows = jnp.concatenate([jnp.full((LRU_W, 1), 0.5, _F32), jnp.ones((POOL_W, 1), _F32)])
    w_out_half = (w_out[l] * gelu_rows).astype(_BF16)
    row = lambda v: v.reshape(1, -1)
    weights = (
        row(g_pre1[l]), w_in[l].astype(_BF16), w_conv[l], row(b_conv[l]),
        wg, row(b_a[l]), row(b_i[l]), row(lam[l]),
        wp, row(pool_scale[l]), w_out_half, row(g_post1[l]), row(g_pre2[l]),
        _chunk_cols(w_up[l].astype(_BF16)), _chunk_cols(w_fconv[l]), _chunk_cols(row(b_fconv[l])),
        (0.5 * w_down[l]).astype(_BF16), row(g_post2[l]),
    )

    yp, cp, hp, pp, fp = _layer_call(
        x_prompt, mod[:nb_p], None, weights,
        prompt=True, seqs_per_tile=1, rows_per_tile=256)

    sconv = jnp.pad(state_conv[l], ((0, 0), (SUBLANES - (CONV_W - 1), 0), (0, 0)))
    spool = jnp.pad(state_pool[l], ((0, 0), (2 * SUBLANES - POOL_BUF, 0), (0, 0)))
    sffn = jnp.pad(state_ffn_conv[l], ((0, 0), (SUBLANES - (FFN_CONV_W - 1), 0), (0, 0)))
    slru = state_lru[l].reshape(nb_s, 1, LRU_W)
    ys, cs, hs, ps, fs = _layer_call(
        x_sample, mod[nb_p:], (sconv, slru, spool, sffn), weights,
        prompt=False, seqs_per_tile=32, rows_per_tile=SUBLANES)

    def unpad(c, h, p, f):
        return (c[None, :, SUBLANES - (CONV_W - 1):],
                h.reshape(1, -1, LRU_W),
                p[None, :, 2 * SUBLANES - POOL_BUF:],
                f[None, :, SUBLANES - (FFN_CONV_W - 1):])

    cp, hp, pp, fp = unpad(cp, hp, pp, fp)
    cs, hs, ps, fs = unpad(cs, hs, ps, fs)
    return (yp, ys, cp, hp, pp, fp, cs, hs, ps, fs)
```

```python
import functools
import math

import jax
import jax.numpy as jnp
from jax import lax
from jax.experimental import pallas as pl
from jax.experimental.pallas import tpu as pltpu

D_MODEL = 1024
LRU_W = 512
LRU_HEADS = 8
LRU_HD = LRU_W // LRU_HEADS
LRU_C = 8.0
CONV_W = 4
POOL_W = 512
POOL_WINDOWS = (2, 4, 8, 16)
POOL_GW = POOL_W // len(POOL_WINDOWS)
POOL_BUF = max(POOL_WINDOWS) - 1
D_FF = 2816
FFN_CONV_W = 3
EPS = 1e-6
PAST_LEN = 16384

SUBLANES = 8
LANES = 128
MXU_DIM = 256
FFN_CHUNK = MXU_DIM
N_CHUNK = D_FF // FFN_CHUNK
HALF = MXU_DIM
VMEM_LIMIT_BYTES = 60 * 1024 * 1024

_BF16 = jnp.bfloat16
_F32 = jnp.float32


def _dot(a, b):
    return jnp.dot(a, b, preferred_element_type=_F32)


def _sigmoid(x):
    return 1.0 / (1.0 + jnp.exp(-x))


def _gelu_x2(x):
    c = math.sqrt(2.0 / math.pi)
    return x * (1.0 + jnp.tanh(x * (c + (c * 0.044715) * (x * x))))


def _rms(x, g):
    ms = jnp.mean(x * x, axis=-1, keepdims=True)
    return x * lax.rsqrt(ms + EPS) * g


def _prev_blocks(e):
    zero = jnp.zeros_like(e[:, :1])
    if e.shape[1] == 1:
        return zero
    return jnp.concatenate([zero, e[:, :-1]], axis=1)


def _shift_rows(e, prev, j, row):
    return pltpu.roll(jnp.where(row < SUBLANES - j, e, prev), j, 2)


def _ada_kernel(c_ref, w_ref, b_ref, o_ref):
    c = c_ref[...]
    a = (c * _sigmoid(c)).astype(_BF16)
    o_ref[...] = _dot(a, w_ref[...].astype(_BF16)) + b_ref[...]


def _ada_call(c_all, w_ada, b_ada):
    n, d = c_all.shape
    e = w_ada.shape[1]
    tn = 1024
    return pl.pallas_call(
        _ada_kernel,
        grid=(e // tn,),
        in_specs=[
            pl.BlockSpec((n, d), lambda j: (0, 0)),
            pl.BlockSpec((d, tn), lambda j: (0, j)),
            pl.BlockSpec((1, tn), lambda j: (0, j)),
        ],
        out_specs=pl.BlockSpec((n, tn), lambda j: (0, j)),
        out_shape=jax.ShapeDtypeStruct((n, e), _F32),
        compiler_params=pltpu.CompilerParams(dimension_semantics=("arbitrary",)),
        name="adaln_mod",
    )(c_all, w_ada, b_ada.reshape(1, e))


def _wup_kernel(act_ref, gate_ref, o_ref):
    o_ref[:, :FFN_CHUNK] = act_ref[...].astype(_BF16)
    o_ref[:, FFN_CHUNK:] = gate_ref[...].astype(_BF16)


def _wup_call(w_up):
    d = w_up.shape[0]
    return pl.pallas_call(
        _wup_kernel,
        grid=(N_CHUNK,),
        in_specs=[
            pl.BlockSpec((d, FFN_CHUNK), lambda c: (0, c)),
            pl.BlockSpec((d, FFN_CHUNK), lambda c: (0, N_CHUNK + c)),
        ],
        out_specs=pl.BlockSpec((d, 2 * FFN_CHUNK), lambda c: (0, c)),
        out_shape=jax.ShapeDtypeStruct((d, 2 * D_FF), _BF16),
        compiler_params=pltpu.CompilerParams(dimension_semantics=("arbitrary",)),
        name="wup_regroup",
    )(w_up, w_up)


def _layer_kernel(*refs, S, CB, prompt, n_tiles, tiles_per_seq):
    if prompt:
        (x_ref, moda_ref, modb_ref,
         gpre1_ref, win_ref, wconv_ref, bconv_ref, wg_ref, ba_ref, bi_ref, lam_ref,
         wp_ref, pscale_ref, wout_ref, gpost1_ref, gpre2_ref,
         wup_ref, wfc_ref, bfc_ref, wdown_ref, gpost2_ref,
         y_ref, oconv_ref, olru_ref, opool_ref, offn_ref,
         hconv_ref, hlru_ref, hpool_ref, hffn_ref,
         mix_ref, x1_ref, h2_ref, acc_ref, up0_ref, up1_ref, f0_ref, f1_ref) = refs
    else:
        (x_ref, moda_ref, sconv_ref, slru_ref, spool_ref, sffn_ref,
         gpre1_ref, win_ref, wconv_ref, bconv_ref, wg_ref, ba_ref, bi_ref, lam_ref,
         wp_ref, pscale_ref, wout_ref, gpost1_ref, gpre2_ref,
         wup_ref, wfc_ref, bfc_ref, wdown_ref, gpost2_ref,
         y_ref, oconv_ref, olru_ref, opool_ref, offn_ref,
         mix_ref, x1_ref, h2_ref, acc_ref, up0_ref, up1_ref, f0_ref, f1_ref) = refs
        modb_ref = moda_ref

    TM = S * CB * SUBLANES
    s = pl.program_id(0)

    if prompt:
        t_a = lax.rem(jnp.minimum(s, n_tiles - 1), tiles_per_seq)
        ffn_seq_start = lax.rem(s + tiles_per_seq - 1, tiles_per_seq) == 0

        @pl.when(s == 0)
        def _():
            x1_ref[...] = jnp.zeros_like(x1_ref)
            h2_ref[...] = jnp.zeros_like(h2_ref)

        @pl.when(lax.rem(s, tiles_per_seq) == 0)
        def _():
            hconv_ref[...] = jnp.zeros_like(hconv_ref)
            hlru_ref[...] = jnp.zeros_like(hlru_ref)
            hpool_ref[...] = jnp.zeros_like(hpool_ref)

        @pl.when(jnp.logical_or(s == 0, ffn_seq_start))
        def _():
            hffn_ref[...] = jnp.zeros_like(hffn_ref)

    def blocks(v2, c):
        return v2.reshape(S, CB, SUBLANES, c)

    def flat(v4):
        return v4.reshape(-1, v4.shape[-1])

    def modv(ref, k):
        return ref[:, k:k + 1, :].reshape(S, 1, 1, D_MODEL)

    def token_mixing(out):
        x4 = blocks(x_ref[...].reshape(TM, D_MODEL), D_MODEL)
        h = _rms(x4, gpre1_ref[...]) * (1.0 + modv(moda_ref, 1)) + modv(moda_ref, 0)
        h_bf = flat(h).astype(_BF16)
        yield
        u = _dot(h_bf, win_ref[...])
        xb = blocks(u[:, :LRU_W], LRU_W)
        gb = blocks(u[:, LRU_W:2 * LRU_W], LRU_W)
        pb = blocks(u[:, 2 * LRU_W:], POOL_W)

        hist_c = hconv_ref[...] if prompt else sconv_ref[...].reshape(S, 1, SUBLANES, LRU_W)
        p_c = hist_c if CB == 1 else jnp.concatenate([hist_c, xb[:, :-1]], axis=1)
        row_c = lax.broadcasted_iota(jnp.int32, xb.shape, 2)
        xc = bconv_ref[...] + wconv_ref[CONV_W - 1:CONV_W, :] * xb
        for j in range(1, CONV_W):
            xc = xc + wconv_ref[CONV_W - 1 - j:CONV_W - j, :] * _shift_rows(xb, p_c, j, row_c)
        if prompt:
            hconv_ref[...] = xb[:, -1:]
        new_conv = xb[:, -1:].reshape(oconv_ref.shape)

        lam = lam_ref[...]
        softplus_neg_lam = jnp.maximum(-lam, 0.0) + jnp.log1p(jnp.exp(-jnp.abs(lam)))
        xc_bf = flat(xc).astype(_BF16)
        row_h = lax.broadcasted_iota(jnp.int32, (S, CB, SUBLANES, HALF), 2)
        new_lru = []
        for hh in range(LRU_W // HALF):
            yield
            sl = slice(hh * HALF, (hh + 1) * HALF)
            g = _dot(xc_bf[:, sl], wg_ref[hh])
            r = _sigmoid(blocks(g[:, :HALF], HALF) + ba_ref[:, sl])
            i = _sigmoid(blocks(g[:, HALF:], HALF) + bi_ref[:, sl])
            log_a = (-LRU_C * softplus_neg_lam[:, sl]) * r
            a = jnp.exp(log_a)
            th = jnp.tanh(log_a)
            m2 = -2.0 * th / (1.0 - th)
            mult = jnp.where(m2 == 0.0, 0.0, m2 * lax.rsqrt(m2))
            b = mult * (i * xc[..., sl])
            yield
            for d in (1, 2, 4):
                a_sh = jnp.where(row_h >= d, pltpu.roll(a, d, 2), 1.0)
                b_sh = jnp.where(row_h >= d, pltpu.roll(b, d, 2), 0.0)
                b = a * b_sh + b
                a = a * a_sh
            if prompt:
                hcar = hlru_ref[:, sl]
                hs = []
                for k in range(CB):
                    hk = a[0, k] * hcar + b[0, k]
                    hs.append(hk)
                    hcar = hk[SUBLANES - 1:SUBLANES, :]
                hr = jnp.stack(hs, axis=0).reshape(S, CB, SUBLANES, HALF)
                hlru_ref[:, sl] = hcar
                new_lru.append(hcar.reshape(1, 1, HALF))
            else:
                h0 = slru_ref[:, :, sl].reshape(S, 1, 1, HALF)
                hr = a * h0 + b
                new_lru.append(pltpu.roll(hr, 1, 2)[:, 0, 0:1, :])
            lru = hr * _gelu_x2(gb[..., sl])
            mix_ref[:, sl] = flat(lru).astype(_BF16)

        yield
        hist_p = hpool_ref[...] if prompt else spool_ref[...].reshape(S, 2, SUBLANES, POOL_W)
        e_p = jnp.concatenate([hist_p, pb], axis=1)
        if prompt:
            hpool_ref[...] = e_p[:, -2:]
        new_pool = e_p[:, -2:].reshape(opool_ref.shape)
        pz_groups = []
        for gi, w in enumerate(POOL_WINDOWS):
            eg = e_p[..., gi * POOL_GW:(gi + 1) * POOL_GW]
            row_g = lax.broadcasted_iota(jnp.int32, eg.shape, 2)
            acc = eg
            span = 1
            while span < w:
                if span < SUBLANES:
                    acc = acc + _shift_rows(acc, _prev_blocks(acc), span, row_g)
                else:
                    acc = acc + _prev_blocks(acc)
                span *= 2
            acc = acc[:, 2:]
            cur = eg[:, 2:]
            if prompt:
                blk = lax.broadcasted_iota(jnp.int32, cur.shape, 1)
                pos = t_a * TM + blk * SUBLANES + row_g[:, 2:]
                inv = 1.0 / jnp.minimum(pos + 1, w).astype(_F32)
            else:
                inv = 1.0 / float(min(PAST_LEN + 1, w))
            pz_groups.append(acc * inv - cur)
        for hh in range(POOL_W // HALF):
            sl = slice(hh * HALF, (hh + 1) * HALF)
            pz = jnp.concatenate(pz_groups[2 * hh:2 * hh + 2], axis=-1)
            q = _dot(flat(pz).astype(_BF16), wp_ref[hh]) * pscale_ref[:, sl]
            mix_ref[:, LRU_W + hh * HALF:LRU_W + (hh + 1) * HALF] = q.astype(_BF16)

        yield
        m = _dot(mix_ref[...], wout_ref[...])
        x1 = x4 + modv(moda_ref, 2) * _rms(blocks(m, D_MODEL), gpost1_ref[...])
        yield
        h2 = _rms(x1, gpre2_ref[...]) * (1.0 + modv(moda_ref, 4)) + modv(moda_ref, 3)
        out.update(x1=x1, h2=h2, conv=new_conv, lru=jnp.concatenate(new_lru, axis=-1),
                   pool=new_pool)

    def conv_ffn(other_work=iter(())):
        acc_ref[...] = jnp.zeros_like(acc_ref)
        row_f = lax.broadcasted_iota(jnp.int32, (S, CB, SUBLANES, 2 * FFN_CHUNK), 2)
        up_refs = (up0_ref, up1_ref)
        f_refs = (f0_ref, f1_ref)

        def stage_u(c, slot):
            cols = slice(c * 2 * FFN_CHUNK, (c + 1) * 2 * FFN_CHUNK)
            up_refs[slot][...] = _dot(h2_ref[...], wup_ref[:, cols])

        def stage_v(c, slot):
            cur = blocks(up_refs[slot][...], 2 * FFN_CHUNK)
            cols = slice(c * 2 * FFN_CHUNK, (c + 1) * 2 * FFN_CHUNK)
            act = slice(c * FFN_CHUNK, (c + 1) * FFN_CHUNK)
            gate = slice(D_FF + c * FFN_CHUNK, D_FF + (c + 1) * FFN_CHUNK)
            if prompt:
                hist_f = hffn_ref[c]
            else:
                hist_f = jnp.concatenate([sffn_ref[:, :, act], sffn_ref[:, :, gate]], axis=-1)
                hist_f = hist_f.reshape(S, 1, SUBLANES, 2 * FFN_CHUNK)
            prev = hist_f if CB == 1 else jnp.concatenate([hist_f, cur[:, :-1]], axis=1)
            upc = bfc_ref[:, cols] + wfc_ref[2:3, cols] * cur
            upc = upc + wfc_ref[1:2, cols] * _shift_rows(cur, prev, 1, row_f)
            upc = upc + wfc_ref[0:1, cols] * _shift_rows(cur, prev, 2, row_f)
            f = _gelu_x2(upc[..., :FFN_CHUNK]) * upc[..., FFN_CHUNK:]
            f_refs[slot][...] = flat(f).astype(_BF16)
            if prompt:
                hffn_ref[c] = cur[:, -1:]
            last = cur[:, -1:].reshape(S, SUBLANES, 2 * FFN_CHUNK)
            offn_ref[:, :, act] = last[..., :FFN_CHUNK]
            offn_ref[:, :, gate] = last[..., FFN_CHUNK:]

        def stage_d(c, slot):
            rows = slice(c * FFN_CHUNK, (c + 1) * FFN_CHUNK)
            acc_ref[...] += _dot(f_refs[slot][...], wdown_ref[rows, :])

        for k in range(N_CHUNK + 2):
            if 1 <= k <= N_CHUNK:
                stage_v(k - 1, (k - 1) % 2)
            if k >= 2:
                stage_d(k - 2, k % 2)
            if k < N_CHUNK:
                stage_u(k, k % 2)
            next(other_work, None)
        for _ in other_work:
            pass

        x1 = blocks(x1_ref[...], D_MODEL)
        y = x1 + modv(modb_ref, 5) * _rms(blocks(acc_ref[...], D_MODEL), gpost2_ref[...])
        y_ref[...] = y.reshape(y_ref.shape)

    out = {}
    if prompt:
        conv_ffn(token_mixing(out))
        x1_ref[...] = flat(out["x1"])
        h2_ref[...] = flat(out["h2"]).astype(_BF16)

        @pl.when(s < n_tiles)
        def _():
            oconv_ref[...] = out["conv"]
            olru_ref[...] = out["lru"]
            opool_ref[...] = out["pool"]
    else:
        for _ in token_mixing(out):
            pass
        x1_ref[...] = flat(out["x1"])
        h2_ref[...] = flat(out["h2"]).astype(_BF16)
        oconv_ref[...] = out["conv"]
        olru_ref[...] = out["lru"]
        opool_ref[...] = out["pool"]
        conv_ffn()


def _const_spec(shape):
    nd = len(shape)
    return pl.BlockSpec(shape, lambda s: (0,) * nd, pipeline_mode=pl.Buffered(1))


def _layer_call(x, mod, states, weights, *, prompt, seqs_per_tile, rows_per_tile):
    B, T, _ = x.shape
    S = seqs_per_tile
    CB = rows_per_tile // SUBLANES
    TM = S * rows_per_tile
    tiles_per_seq = T // rows_per_tile
    n_tiles = (B // S) * tiles_per_seq

    if prompt:
        assert S == 1
        grid = (n_tiles + 1,)

        def tile_a(s):
            return jnp.minimum(s, n_tiles - 1)

        def tile_b(s):
            return jnp.maximum(s - 1, 0)

        x_map = lambda s: (tile_a(s) // tiles_per_seq, tile_a(s) % tiles_per_seq, 0)
        y_map = lambda s: (tile_b(s) // tiles_per_seq, tile_b(s) % tiles_per_seq, 0)
        seq_a = lambda s: (tile_a(s) // tiles_per_seq, 0, 0)
        seq_b = lambda s: (tile_b(s) // tiles_per_seq, 0, 0)
    else:
        assert tiles_per_seq == 1
        grid = (n_tiles,)
        x_map = y_map = seq_a = seq_b = lambda s: (s, 0, 0)

    in_specs = [
        pl.BlockSpec((S, rows_per_tile, D_MODEL), x_map),
        pl.BlockSpec((S, 6, D_MODEL), seq_a),
    ]
    args = [x, mod]
    if prompt:
        in_specs.append(pl.BlockSpec((S, 6, D_MODEL), seq_b))
        args.append(mod)
    else:
        sconv, slru, spool, sffn = states
        in_specs += [
            pl.BlockSpec((S, SUBLANES, LRU_W), seq_a),
            pl.BlockSpec((S, 1, LRU_W), seq_a),
            pl.BlockSpec((S, 2 * SUBLANES, POOL_W), seq_a),
            pl.BlockSpec((S, SUBLANES, 2 * D_FF), seq_a),
        ]
        args += [sconv, slru, spool, sffn]
    in_specs += [_const_spec(w.shape) for w in weights]
    args += list(weights)

    out_shape = (
        jax.ShapeDtypeStruct((B, T, D_MODEL), _F32),
        jax.ShapeDtypeStruct((B, SUBLANES, LRU_W), _F32),
        jax.ShapeDtypeStruct((B, 1, LRU_W), _F32),
        jax.ShapeDtypeStruct((B, 2 * SUBLANES, POOL_W), _F32),
        jax.ShapeDtypeStruct((B, SUBLANES, 2 * D_FF), _F32),
    )
    out_specs = (
        pl.BlockSpec((S, rows_per_tile, D_MODEL), y_map),
        pl.BlockSpec((S, SUBLANES, LRU_W), seq_a),
        pl.BlockSpec((S, 1, LRU_W), seq_a),
        pl.BlockSpec((S, 2 * SUBLANES, POOL_W), seq_a),
        pl.BlockSpec((S, SUBLANES, 2 * D_FF), seq_b),
    )
    scratch = []
    if prompt:
        scratch += [
            pltpu.VMEM((1, 1, SUBLANES, LRU_W), _F32),
            pltpu.VMEM((1, LRU_W), _F32),
            pltpu.VMEM((1, 2, SUBLANES, POOL_W), _F32),
            pltpu.VMEM((N_CHUNK, 1, 1, SUBLANES, 2 * FFN_CHUNK), _F32),
        ]
    scratch += [
        pltpu.VMEM((TM, LRU_W + POOL_W), _BF16),
        pltpu.VMEM((TM, D_MODEL), _F32),
        pltpu.VMEM((TM, D_MODEL), _BF16),
        pltpu.VMEM((TM, D_MODEL), _F32),
        pltpu.VMEM((TM, 2 * FFN_CHUNK), _F32),
        pltpu.VMEM((TM, 2 * FFN_CHUNK), _F32),
        pltpu.VMEM((TM, FFN_CHUNK), _BF16),
        pltpu.VMEM((TM, FFN_CHUNK), _BF16),
    ]
    return pl.pallas_call(
        functools.partial(_layer_kernel, S=S, CB=CB, prompt=prompt,
                          n_tiles=n_tiles, tiles_per_seq=tiles_per_seq),
        grid=grid,
        in_specs=in_specs,
        out_specs=out_specs,
        out_shape=out_shape,
        scratch_shapes=scratch,
        compiler_params=pltpu.CompilerParams(
            dimension_semantics=("arbitrary",),
            vmem_limit_bytes=VMEM_LIMIT_BYTES),
        name="layer_prompt" if prompt else "layer_sample",
    )(*args)


def _block_diag(blocks):
    n, k, _ = blocks.shape
    eye = jnp.eye(n, dtype=blocks.dtype)
    return (eye[:, None, :, None] * blocks[:, :, None, :]).reshape(n * k, n * k)


def _chunk_cols(v):
    v4 = v.reshape(v.shape[:-1] + (2, N_CHUNK, FFN_CHUNK))
    return jnp.swapaxes(v4, -3, -2).reshape(v.shape)


def kernel(x_prompt, x_sample, c_prompt, c_sample, state_conv, state_lru, state_pool, state_ffn_conv, w_ada, b_ada, g_pre1, w_in, w_conv, b_conv, w_a, b_a, w_i, b_i, lam, w_pool, pool_scale, w_out, g_post1, g_pre2, w_up, w_fconv, b_fconv, w_down, g_post2):
    l = 0
    nb_p = x_prompt.shape[0]
    nb_s = x_sample.shape[0]

    mod = _ada_call(jnp.concatenate([c_prompt, c_sample], axis=0), w_ada[l], b_ada[l])
    mod = mod.reshape(nb_p + nb_s, 6, D_MODEL)

    heads_per_half = HALF // LRU_HD
    groups_per_half = HALF // POOL_GW
    wg = jnp.stack([
        jnp.concatenate([
            _block_diag(w_a[l, hh * heads_per_half:(hh + 1) * heads_per_half]),
            _block_diag(w_i[l, hh * heads_per_half:(hh + 1) * heads_per_half])], axis=1)
        for hh in range(LRU_W // HALF)]).astype(_BF16)
    wp = jnp.stack([
        _block_diag(w_pool[l, hh * groups_per_half:(hh + 1) * groups_per_half])
        for hh in range(POOL_W // HALF)]).astype(_BF16)
    gelu_rows = jnp.concatenate([jnp.full((LRU_W, 1), 0.5, _F32), jnp.ones((POOL_W, 1), _F32)])
    w_out_half = (w_out[l] * gelu_rows).astype(_BF16)
    row = lambda v: v.reshape(1, -1)
    weights = (
        row(g_pre1[l]), w_in[l].astype(_BF16), w_conv[l], row(b_conv[l]),
        wg, row(b_a[l]), row(b_i[l]), row(lam[l]),
        wp, row(pool_scale[l]), w_out_half, row(g_post1[l]), row(g_pre2[l]),
        _wup_call(w_up[l]), _chunk_cols(w_fconv[l]), _chunk_cols(row(b_fconv[l])),
        (0.5 * w_down[l]).astype(_BF16), row(g_post2[l]),
    )

    yp, cp, hp, pp, fp = _layer_call(
        x_prompt, mod[:nb_p], None, weights,
        prompt=True, seqs_per_tile=1, rows_per_tile=256)

    sconv = jnp.pad(state_conv[l], ((0, 0), (SUBLANES - (CONV_W - 1), 0), (0, 0)))
    spool = jnp.pad(state_pool[l], ((0, 0), (2 * SUBLANES - POOL_BUF, 0), (0, 0)))
    sffn = jnp.pad(state_ffn_conv[l], ((0, 0), (SUBLANES - (FFN_CONV_W - 1), 0), (0, 0)))
    slru = state_lru[l].reshape(nb_s, 1, LRU_W)
    ys, cs, hs, ps, fs = _layer_call(
        x_sample, mod[nb_p:], (sconv, slru, spool, sffn), weights,
        prompt=False, seqs_per_tile=32, rows_per_tile=SUBLANES)

    def unpad(c, h, p, f):
        return (c[None, :, SUBLANES - (CONV_W - 1):],
                h.reshape(1, -1, LRU_W),
                p[None, :, 2 * SUBLANES - POOL_BUF:],
                f[None, :, SUBLANES - (FFN_CONV_W - 1):])

    cp, hp, pp, fp = unpad(cp, hp, pp, fp)
    cs, hs, ps, fs = unpad(cs, hs, ps, fs)
    return (yp, ys, cp, hp, pp, fp, cs, hs, ps, fs)
```
